```python
import math
import jax, jax.numpy as jnp
from jax import lax
import numpy as np

D_MODEL = 1024
BATCH = 8
SEQ = 4096
DEPTH = 1

GRID_W = 64
N_HEADS = 8
N_KV_HEADS = 2
HEAD_DIM = 64
ROPE_THETA = 10000.0
Q_BLOCK = 128
SSD_D_INNER = 1024
SSD_HEAD_DIM = 64
SSD_N_HEADS = SSD_D_INNER // SSD_HEAD_DIM
SSD_GROUPS = 2
SSD_HEADS_PER_GROUP = SSD_N_HEADS // SSD_GROUPS
SSD_D_STATE = 128
SSD_CONV_W = 5
SSD_CHUNK = 128
PEER_HEADS = 8
PEER_N_KEYS = 128
PEER_N_EXPERTS = PEER_N_KEYS * PEER_N_KEYS
PEER_D_QUERY = 256
PEER_D_HALF = PEER_D_QUERY // 2
PEER_TOPK = 16
PEER_TOKEN_BLOCK = 128
N_BRANCH = 2
EPS = 1e-6

ATT_Q_W = N_HEADS * HEAD_DIM
ATT_KV_W = N_KV_HEADS * HEAD_DIM
SSD_XBC_W = SSD_D_INNER + 2 * SSD_GROUPS * SSD_D_STATE
GATE_W = N_BRANCH * D_MODEL
IN_W = ATT_Q_W + 2 * ATT_KV_W + SSD_D_INNER + SSD_XBC_W + 2 * SSD_N_HEADS + GATE_W

kernel_name = 'hybrid_gqa_ssd_peer_encoder_block'


def rmsnorm(x, w):
    xf = x.astype(jnp.float32)
    y = xf * lax.rsqrt(jnp.mean(xf * xf, axis=-1, keepdims=True) + EPS)
    return (y * w.astype(jnp.float32)).astype(x.dtype)


def modulate(h, shift, scale):
    return h * (1.0 + scale[:, None, :]) + shift[:, None, :]


def _rotate(x, ang):
    d2 = x.shape[-1] // 2
    x1, x2 = x[..., :d2], x[..., d2:]
    cos, sin = jnp.cos(ang), jnp.sin(ang)
    return jnp.concatenate([x1 * cos - x2 * sin, x1 * sin + x2 * cos], axis=-1)


def axial_rope(x, row, col):
    half = HEAD_DIM // 2
    inv = ROPE_THETA ** (-jnp.arange(0, half, 2, dtype=jnp.float32) / half)
    xf = x.astype(jnp.float32)
    ang_r = row.astype(jnp.float32)[:, None] * inv
    ang_c = col.astype(jnp.float32)[:, None] * inv
    out = jnp.concatenate([_rotate(xf[..., :half], ang_r), _rotate(xf[..., half:], ang_c)], axis=-1)
    return out.astype(x.dtype)


def block_attention(q, k, v):
    b, H, S, hd = q.shape
    G = H // N_KV_HEADS
    nb = S // Q_BLOCK
    qb = q.reshape(b, N_KV_HEADS, G, nb, Q_BLOCK, hd).transpose(3, 0, 1, 2, 4, 5)
    scale = hd ** -0.5

    def one(qi):
        s = jnp.einsum('bkgqd,bksd->bkgqs', qi, k).astype(jnp.float32) * scale
        p = jax.nn.softmax(s, axis=-1).astype(v.dtype)
        return jnp.einsum('bkgqs,bksd->bkgqd', p, v)

    o = lax.map(one, qb)
    return o.transpose(1, 0, 4, 2, 3, 5).reshape(b, S, H * hd)


def depthwise_conv(u, w, bias):
    C = u.shape[-1]
    pad = (SSD_CONV_W - 1) // 2
    y = lax.conv_general_dilated(u, w[:, None, :].astype(u.dtype), (1,), [(pad, pad)],
                                 dimension_numbers=('NWC', 'WIO', 'NWC'), feature_group_count=C)
    return y + bias


def segsum(x):
    T = x.shape[-1]
    xe = jnp.broadcast_to(x[..., None], x.shape + (T,))
    strict = jnp.tril(jnp.ones((T, T), dtype=bool), -1)
    xe = jnp.where(strict, xe, 0.0)
    cs = jnp.cumsum(xe, axis=-2)
    return jnp.where(jnp.tril(jnp.ones((T, T), dtype=bool), 0), cs, -jnp.inf)


def ssd_causal(xh, dt, A, Bm, Cm):
    b, S, g, r, p = xh.shape
    n = Bm.shape[-1]
    nc, l = S // SSD_CHUNK, SSD_CHUNK
    X = (xh * dt[..., None]).reshape(b, nc, l, g, r, p)
    Adt = (dt * A).reshape(b, nc, l, g, r).transpose(0, 3, 4, 1, 2)
    Bc = Bm.reshape(b, nc, l, g, n)
    Cc = Cm.reshape(b, nc, l, g, n)
    A_cs = jnp.cumsum(Adt, axis=-1)
    Lmat = jnp.exp(segsum(Adt))
    cb = jnp.einsum('bclgn,bcsgn->bgcls', Cc, Bc)
    y_diag = jnp.einsum('bgrcls,bcsgrp->bclgrp', cb[:, :, None] * Lmat, X)
    decay_states = jnp.exp(A_cs[..., -1:] - A_cs)
    states = jnp.einsum('bclgn,bgrcl,bclgrp->bcgrpn', Bc, decay_states, X)
    states = jnp.concatenate([jnp.zeros_like(states[:, :1]), states], axis=1)
    last = jnp.pad(A_cs[..., -1], ((0, 0), (0, 0), (0, 0), (1, 0)))
    decay_chunk = jnp.exp(segsum(last))
    new_states = jnp.einsum('bgrzc,bcgrpn->bzgrpn', decay_chunk, states)
    states = new_states[:, :-1]
    y_off = jnp.einsum('bclgn,bcgrpn,bgrcl->bclgrp', Cc, states, jnp.exp(A_cs))
    return (y_diag + y_off).reshape(b, S, g, r, p)


def ssd_branch(z, xbc, dtf, dtb, conv_w, conv_b, dt_bias_f, dt_bias_b, a_log_f, a_log_b, d_skip, ssd_norm_w):
    b, S, _ = z.shape
    G, R, P, N = SSD_GROUPS, SSD_HEADS_PER_GROUP, SSD_HEAD_DIM, SSD_D_STATE
    xbc = jax.nn.silu(depthwise_conv(xbc, conv_w, conv_b))
    xs, Bm, Cm = jnp.split(xbc, [SSD_D_INNER, SSD_D_INNER + G * N], axis=-1)
    xh = xs.astype(jnp.float32).reshape(b, S, G, R, P)
    Bm = Bm.astype(jnp.float32).reshape(b, S, G, N)
    Cm = Cm.astype(jnp.float32).reshape(b, S, G, N)

    def disc(raw, bias, a_log):
        dt = jax.nn.softplus(raw.astype(jnp.float32) + bias.astype(jnp.float32)).reshape(b, S, G, R)
        A = -jnp.exp(a_log.astype(jnp.float32)).reshape(G, R)
        return dt, A

    dt_f, A_f = disc(dtf, dt_bias_f, a_log_f)
    dt_b, A_b = disc(dtb, dt_bias_b, a_log_b)
    flip = lambda t: jnp.flip(t, axis=1)
    y_f = ssd_causal(xh, dt_f, A_f, Bm, Cm)
    y_b = flip(ssd_causal(flip(xh), flip(dt_b), A_b, flip(Bm), flip(Cm)))
    y = y_f + y_b + xh * d_skip.astype(jnp.float32).reshape(G, R)[:, :, None]
    y = y.reshape(b, S, SSD_D_INNER).astype(z.dtype)
    return rmsnorm(y * jax.nn.silu(z), ssd_norm_w)


def token_mixer(h, row, col, w_in, q_gain, k_gain, conv_w, conv_b, dt_bias_f, dt_bias_b,
                a_log_f, a_log_b, d_skip, ssd_norm_w, w_attn_up, w_ssd_up, w_out):
    b, S, _ = h.shape
    proj = h @ w_in
    o1 = ATT_Q_W
    o2 = o1 + ATT_KV_W
    o3 = o2 + ATT_KV_W
    o4 = o3 + SSD_D_INNER
    o5 = o4 + SSD_XBC_W
    o6 = o5 + SSD_N_HEADS
    o7 = o6 + SSD_N_HEADS
    q, k, v, z, xbc, dtf, dtb, gl = jnp.split(proj, [o1, o2, o3, o4, o5, o6, o7], axis=-1)
    q = q.reshape(b, S, N_HEADS, HEAD_DIM).transpose(0, 2, 1, 3)
    k = k.reshape(b, S, N_KV_HEADS, HEAD_DIM).transpose(0, 2, 1, 3)
    v = v.reshape(b, S, N_KV_HEADS, HEAD_DIM).transpose(0, 2, 1, 3)
    q = axial_rope(rmsnorm(q, q_gain), row, col)
    k = axial_rope(rmsnorm(k, k_gain), row, col)
    att = block_attention(q, k, v)
    ssd = ssd_branch(z, xbc, dtf, dtb, conv_w, conv_b, dt_bias_f, dt_bias_b,
                     a_log_f, a_log_b, d_skip, ssd_norm_w)
    gates = jax.nn.sigmoid(gl.astype(jnp.float32)).astype(h.dtype).reshape(b, S, N_BRANCH, D_MODEL)
    merged = gates[:, :, 0] * (att @ w_attn_up) + gates[:, :, 1] * (ssd @ w_ssd_up)
    return merged @ w_out


def peer(h, w_query, keys1, keys2, u_tab, v_tab):
    b, S, D = h.shape
    xb = h.reshape(-1, PEER_TOKEN_BLOCK, D)

    def one(xt):
        T = xt.shape[0]
        q = (xt @ w_query).reshape(T, PEER_HEADS, 2, PEER_D_HALF)
        s1 = jnp.einsum('thd,hkd->thk', q[:, :, 0], keys1).astype(jnp.float32)
        s2 = jnp.einsum('thd,hkd->thk', q[:, :, 1], keys2).astype(jnp.float32)
        v1, i1 = lax.top_k(s1, PEER_TOPK)
        v2, i2 = lax.top_k(s2, PEER_TOPK)
        cand = (v1[..., :, None] + v2[..., None, :]).reshape(T, PEER_HEADS, PEER_TOPK * PEER_TOPK)
        cidx = (i1[..., :, None] * PEER_N_KEYS + i2[..., None, :]).reshape(T, PEER_HEADS, PEER_TOPK * PEER_TOPK)
        sc, pos = lax.top_k(cand, PEER_TOPK)
        idx = jnp.take_along_axis(cidx, pos, axis=-1)
        g = jax.nn.softmax(sc, axis=-1)
        u = u_tab[idx]
        a = jax.nn.gelu(jnp.einsum('thkd,td->thk', u, xt).astype(jnp.float32))
        vv = v_tab[idx]
        return jnp.einsum('thk,thkd->td', (g * a).astype(vv.dtype), vv)

    return lax.map(one, xb).reshape(b, S, D)


def setup_inputs(seed: int = 0) -> dict:
    key = jax.random.key(seed)
    ks = jax.random.split(key, 32)
    f = jnp.float32
    L, D = DEPTH, D_MODEL

    def nrm(k, shape, scale):
        return jax.random.normal(k, shape, f) * scale

    dt_f0 = jnp.exp(jax.random.uniform(ks[10], (L, SSD_N_HEADS), f, math.log(1e-3), math.log(1e-1)))
    dt_b0 = jnp.exp(jax.random.uniform(ks[11], (L, SSD_N_HEADS), f, math.log(1e-3), math.log(1e-1)))
    return {
        'x': nrm(ks[0], (BATCH, SEQ, D), 1.0),
        'c': nrm(ks[1], (BATCH, D), 1.0),
        'ada_w': nrm(ks[2], (L, D, 6 * D), D ** -0.5),
        'ada_b': nrm(ks[3], (L, 6 * D), 0.02),
        'norm1_w': 1.0 + nrm(ks[4], (L, D), 0.02),
        'w_in': nrm(ks[5], (L, D, IN_W), D ** -0.5),
        'q_gain': 1.0 + nrm(ks[6], (L, HEAD_DIM), 0.02),
        'k_gain': 1.0 + nrm(ks[7], (L, HEAD_DIM), 0.02),
        'conv_w': nrm(ks[8], (L, SSD_CONV_W, SSD_XBC_W), SSD_CONV_W ** -0.5),
        'conv_b': nrm(ks[9], (L, SSD_XBC_W), 0.02),
        'dt_bias_f': dt_f0 + jnp.log(-jnp.expm1(-dt_f0)),
        'dt_bias_b': dt_b0 + jnp.log(-jnp.expm1(-dt_b0)),
        'a_log_f': jnp.log(jax.random.uniform(ks[12], (L, SSD_N_HEADS), f, 1.0, 16.0)),
        'a_log_b': jnp.log(jax.random.uniform(ks[13], (L, SSD_N_HEADS), f, 1.0, 16.0)),
        'd_skip': 1.0 + nrm(ks[14], (L, SSD_N_HEADS), 0.02),
        'ssd_norm_w': 1.0 + nrm(ks[15], (L, SSD_D_INNER), 0.02),
        'w_attn_up': nrm(ks[16], (L, ATT_Q_W, D), ATT_Q_W ** -0.5),
        'w_ssd_up': nrm(ks[17], (L, SSD_D_INNER, D), SSD_D_INNER ** -0.5),
        'w_out': nrm(ks[18], (L, D, D), D ** -0.5),
        'norm2_w': 1.0 + nrm(ks[19], (L, D), 0.02),
        'peer_w_query': nrm(ks[20], (L, D, PEER_HEADS * PEER_D_QUERY), D ** -0.5),
        'peer_keys1': nrm(ks[21], (L, PEER_HEADS, PEER_N_KEYS, PEER_D_HALF), PEER_D_HALF ** -0.5),
        'peer_keys2': nrm(ks[22], (L, PEER_HEADS, PEER_N_KEYS, PEER_D_HALF), PEER_D_HALF ** -0.5),
        'peer_u': nrm(ks[23], (L, PEER_N_EXPERTS, D), D ** -0.5),
        'peer_v': nrm(ks[24], (L, PEER_N_EXPERTS, D), PEER_HEADS ** -0.5),
        'final_norm_w': 1.0 + nrm(ks[25], (D,), 0.02),
    }


def reference(x, c, ada_w, ada_b, norm1_w, w_in, q_gain, k_gain, conv_w, conv_b,
              dt_bias_f, dt_bias_b, a_log_f, a_log_b, d_skip, ssd_norm_w,
              w_attn_up, w_ssd_up, w_out, norm2_w, peer_w_query, peer_keys1,
              peer_keys2, peer_u, peer_v, final_norm_w):
    S = x.shape[1]
    rows = S // GRID_W
    row = jnp.repeat(jnp.arange(rows, dtype=jnp.int32), GRID_W)
    col = jnp.tile(jnp.arange(GRID_W, dtype=jnp.int32), rows)
    c_act = jax.nn.silu(c)
    for l in range(DEPTH):
        mod = c_act @ ada_w[l] + ada_b[l]
        sh1, sc1, g1, sh2, sc2, g2 = jnp.split(mod, 6, axis=-1)
        h = modulate(rmsnorm(x, norm1_w[l]), sh1, sc1)
        mix = token_mixer(h, row, col, w_in[l], q_gain[l], k_gain[l], conv_w[l], conv_b[l],
                          dt_bias_f[l], dt_bias_b[l], a_log_f[l], a_log_b[l], d_skip[l],
                          ssd_norm_w[l], w_attn_up[l], w_ssd_up[l], w_out[l])
        x = x + g1[:, None, :] * mix
        h = modulate(rmsnorm(x, norm2_w[l]), sh2, sc2)
        x = x + g2[:, None, :] * peer(h, peer_w_query[l], peer_keys1[l], peer_keys2[l], peer_u[l], peer_v[l])
    return rmsnorm(x, final_norm_w)
```

```python
import functools
import math

import jax
import jax.numpy as jnp
from jax import lax
from jax.experimental import pallas as pl
from jax.experimental.pallas import tpu as pltpu

F32 = jnp.float32
BF16 = jnp.bfloat16

EPS = 1e-6
GRID_W = 64
ROPE_THETA = 10000.0
N_HEADS = 8
N_KV_HEADS = 2
HEAD_DIM = 64
SSD_HEADS = 16
SSD_HEAD_DIM = 64
SSD_GROUPS = 2
SSD_STATE = 128
SSD_CHUNK = 128
CONV_W = 5
PEER_HEADS = 8
PEER_KEYS = 128
PEER_TOPK = 16
D_MODEL = 1024

VMEM_LIMIT_BYTES = 56 * 1024 * 1024

COL_Z = 0
COL_GATE = 1024
COL_XBC = 3072
COL_Q = 4608
COL_K = 5120
COL_V = 5248
COL_DT = 5376
PROJ_W = 5632


def _cparams(*sem):
    return pltpu.CompilerParams(dimension_semantics=sem, vmem_limit_bytes=VMEM_LIMIT_BYTES)


def _split_bf16(x, n):
    parts = []
    r = x
    for _ in range(n):
        p = r.astype(BF16)
        parts.append(p)
        r = r - p.astype(F32)
    return parts


def _sigmoid(x):
    return 1.0 / (1.0 + jnp.exp(-x))


def _softplus(x):
    return jnp.maximum(x, 0.0) + jnp.log1p(jnp.exp(-jnp.abs(x)))


def _rms_scale(x):
    return x * lax.rsqrt(jnp.mean(x * x, axis=-1, keepdims=True) + EPS)


def _adaln_kernel(c_ref, w_ref, b_ref, o_ref):
    c = c_ref[...]
    act = c * _sigmoid(c)
    o_ref[...] = jnp.dot(act, w_ref[...], preferred_element_type=F32,
                         precision=lax.Precision.HIGHEST) + b_ref[...]


def _adaln(c, ada_w, ada_b):
    B, D = c.shape
    N = ada_w.shape[1]
    tn = 512
    return pl.pallas_call(
        _adaln_kernel,
        out_shape=jax.ShapeDtypeStruct((B, N), F32),
        grid=(N // tn,),
        in_specs=[pl.BlockSpec((B, D), lambda j: (0, 0)),
                  pl.BlockSpec((D, tn), lambda j: (0, j)),
                  pl.BlockSpec((1, tn), lambda j: (0, j))],
        out_specs=pl.BlockSpec((B, tn), lambda j: (0, j)),
        compiler_params=_cparams("arbitrary"),
        name="adaln",
    )(c, ada_w, ada_b.reshape(1, N))


def _in_proj_kernel(x_ref, nw_ref, sh_ref, sc_ref, w_ref, o_ref, h_scr):
    @pl.when(pl.program_id(1) == 0)
    def _():
        y = _rms_scale(x_ref[...]) * nw_ref[...]
        h_scr[...] = (y * (1.0 + sc_ref[0]) + sh_ref[0]).astype(BF16)

    o_ref[...] = jnp.dot(h_scr[...], w_ref[...], preferred_element_type=F32)


def _in_proj(x2, norm_w, shift, scale, w_bf16, S):
    N, D = x2.shape
    W = w_bf16.shape[1]
    tm = min(1024, S)
    tn = 512
    per_b = S // tm
    return pl.pallas_call(
        _in_proj_kernel,
        out_shape=jax.ShapeDtypeStruct((N, W), F32),
        grid=(N // tm, W // tn),
        in_specs=[pl.BlockSpec((tm, D), lambda i, j: (i, 0)),
                  pl.BlockSpec((1, D), lambda i, j: (0, 0)),
                  pl.BlockSpec((1, 1, D), lambda i, j: (i // per_b, 0, 0)),
                  pl.BlockSpec((1, 1, D), lambda i, j: (i // per_b, 0, 0)),
                  pl.BlockSpec((D, tn), lambda i, j: (0, j))],
        out_specs=pl.BlockSpec((tm, tn), lambda i, j: (i, j)),
        scratch_shapes=[pltpu.VMEM((tm, D), BF16)],
        compiler_params=_cparams("parallel", "arbitrary"),
        name="in_proj",
    )(x2, norm_w.reshape(1, D), shift, scale, w_bf16)


def _group_meansq(x, gmat):
    hi, lo = _split_bf16(x * x, 2)
    return (jnp.dot(hi, gmat, preferred_element_type=F32)
            + jnp.dot(lo, gmat, preferred_element_type=F32))


def _rope(x, cos, sin_a, sin_b):
    n = x.shape[1]
    return x * cos + pltpu.roll(x, n - 16, 1) * sin_a + pltpu.roll(x, 16, 1) * sin_b


def _attn_kernel(q_ref, k_ref, v_ref, cos_ref, sa_ref, sb_ref, gq_ref, gk_ref,
                 qg_ref, kg_ref, o_ref, kt_scr, v_scr, *, tq):
    i = pl.program_id(1)
    n_q = N_HEADS // N_KV_HEADS

    @pl.when(i == 0)
    def _():
        k = k_ref[...]
        kn = k * lax.rsqrt(_group_meansq(k, gk_ref[...]) + EPS) * kg_ref[...]
        kr = _rope(kn, cos_ref[...], sa_ref[...], sb_ref[...])
        kt_scr[...] = kr.T.astype(BF16)
        v = v_ref[...]
        for g in range(N_KV_HEADS):
            v_scr[g] = v[:, g * HEAD_DIM:(g + 1) * HEAD_DIM].astype(BF16)

    rows = pl.ds(pl.multiple_of(i * tq, tq), tq)
    tile = lambda t: jnp.concatenate([t] * (N_HEADS // 2), axis=1)
    q = q_ref[...]
    qn = q * lax.rsqrt(_group_meansq(q, gq_ref[...]) + EPS) * (qg_ref[...] * (HEAD_DIM ** -0.5))
    qr = _rope(qn, tile(cos_ref[rows, :]), tile(sa_ref[rows, :]), tile(sb_ref[rows, :]))
    qb = qr.astype(BF16)
    outs = []
    for g in range(N_KV_HEADS):
        qs = jnp.concatenate(
            [qb[:, (n_q * g + r) * HEAD_DIM:(n_q * g + r + 1) * HEAD_DIM] for r in range(n_q)], axis=0)
        s = jnp.dot(qs, kt_scr[g * HEAD_DIM:(g + 1) * HEAD_DIM, :], preferred_element_type=F32)
        m = jnp.max(s, axis=-1, keepdims=True)
        p = jnp.exp(s - m)
        l = jnp.sum(p, axis=-1, keepdims=True)
        o = jnp.dot(p.astype(BF16), v_scr[g], preferred_element_type=F32) / l
        for r in range(n_q):
            outs.append(o[r * tq:(r + 1) * tq, :])
    o_ref[...] = jnp.concatenate(outs, axis=1)


def _rope_tables(S):
    half = HEAD_DIM // 2
    inv = ROPE_THETA ** (-jnp.arange(0, half, 2, dtype=F32) / half)
    t = jnp.arange(S, dtype=jnp.int32)
    ang_r = (t // GRID_W).astype(F32)[:, None] * inv
    ang_c = (t % GRID_W).astype(F32)[:, None] * inv
    zero = jnp.zeros_like(ang_r)
    cos = jnp.concatenate([jnp.cos(ang_r)] * 2 + [jnp.cos(ang_c)] * 2, axis=1)
    sin_a = jnp.concatenate([-jnp.sin(ang_r), zero, -jnp.sin(ang_c), zero], axis=1)
    sin_b = jnp.concatenate([zero, jnp.sin(ang_r), zero, jnp.sin(ang_c)], axis=1)
    two = lambda a: jnp.concatenate([a, a], axis=1)
    return two(cos), two(sin_a), two(sin_b)


def _group_matrix(n, group):
    idx = jnp.arange(n) // group
    return jnp.where(idx[:, None] == idx[None, :], 1.0 / group, 0.0).astype(BF16)


def _attention(proj3, q_gain, k_gain):
    B, S, _ = proj3.shape
    tq = 128
    cos, sin_a, sin_b = _rope_tables(S)
    qw = N_HEADS * HEAD_DIM
    kw = N_KV_HEADS * HEAD_DIM
    const = lambda shape: pl.BlockSpec(shape, lambda b, i: (0,) * len(shape))
    return pl.pallas_call(
        functools.partial(_attn_kernel, tq=tq),
        out_shape=jax.ShapeDtypeStruct((B, S, qw), F32),
        grid=(B, S // tq),
        in_specs=[pl.BlockSpec((None, tq, qw), lambda b, i: (b, i, COL_Q // qw)),
                  pl.BlockSpec((None, S, kw), lambda b, i: (b, 0, COL_K // kw)),
                  pl.BlockSpec((None, S, kw), lambda b, i: (b, 0, COL_V // kw)),
                  const((S, kw)), const((S, kw)), const((S, kw)),
                  const((qw, qw)), const((kw, kw)), const((1, qw)), const((1, kw))],
        out_specs=pl.BlockSpec((None, tq, qw), lambda b, i: (b, i, 0)),
        scratch_shapes=[pltpu.VMEM((kw, S), BF16),
                        pltpu.VMEM((N_KV_HEADS, S, HEAD_DIM), BF16)],
        compiler_params=_cparams("parallel", "arbitrary"),
        name="attention",
    )(proj3, proj3, proj3, cos, sin_a, sin_b,
      _group_matrix(qw, HEAD_DIM), _group_matrix(kw, HEAD_DIM),
      jnp.tile(q_gain, N_HEADS).reshape(1, qw), jnp.tile(k_gain, N_KV_HEADS).reshape(1, kw))


def _conv_kernel(prev_ref, main_ref, next_ref, cw_ref, cb_ref, xs_ref, bc_ref, ext_scr, *, ts):
    i = pl.program_id(1)
    last = pl.num_programs(1) - 1
    ext_scr[0:8, :] = jnp.where(i > 0, prev_ref[...], 0.0)
    ext_scr[8:8 + ts, :] = main_ref[...]
    ext_scr[8 + ts:16 + ts, :] = jnp.where(i < last, next_ref[...], 0.0)
    pad = (CONV_W - 1) // 2
    acc = cb_ref[...] + ext_scr[pl.ds(8 - pad, ts), :] * cw_ref[0:1, :]
    for w in range(1, CONV_W):
        acc = acc + ext_scr[pl.ds(8 - pad + w, ts), :] * cw_ref[w:w + 1, :]
    u = acc * _sigmoid(acc)
    d_inner = xs_ref.shape[-1]
    xs_ref[...] = u[:, :d_inner]
    bc_ref[...] = u[:, d_inner:]


def _conv_silu(proj3, conv_w, conv_b):
    B, S, _ = proj3.shape
    C = conv_w.shape[1]
    d_inner = SSD_HEADS * SSD_HEAD_DIM
    ts = min(512, S)
    nb8 = S // 8
    cb = COL_XBC // C
    return pl.pallas_call(
        functools.partial(_conv_kernel, ts=ts),
        out_shape=(jax.ShapeDtypeStruct((B, S, d_inner), F32),
                   jax.ShapeDtypeStruct((B, S, C - d_inner), F32)),
        grid=(B, S // ts),
        in_specs=[pl.BlockSpec((None, 8, C), lambda b, i: (b, jnp.maximum(i * (ts // 8) - 1, 0), cb)),
                  pl.BlockSpec((None, ts, C), lambda b, i: (b, i, cb)),
                  pl.BlockSpec((None, 8, C), lambda b, i: (b, jnp.minimum((i + 1) * (ts // 8), nb8 - 1), cb)),
                  pl.BlockSpec((CONV_W, C), lambda b, i: (0, 0)),
                  pl.BlockSpec((1, C), lambda b, i: (0, 0))],
        out_specs=(pl.BlockSpec((None, ts, d_inner), lambda b, i: (b, i, 0)),
                   pl.BlockSpec((None, ts, C - d_inner), lambda b, i: (b, i, 0))),
        scratch_shapes=[pltpu.VMEM((ts + 16, C), F32)],
        compiler_params=_cparams("parallel", "arbitrary"),
        name="conv_silu",
    )(proj3, proj3, proj3, conv_w, conv_b.reshape(1, C))


def _ssd_direction(xs, bc, dt_raw, dtT_raw, bias_r, alog_r, bias_c, alog_c, h_ref, reverse):
    L = SSD_CHUNK
    hp = SSD_HEADS // SSD_GROUPS * SSD_HEAD_DIM
    dt = _softplus(dt_raw + bias_r)
    a = dt * (-jnp.exp(alog_r))
    dtT = _softplus(dtT_raw + bias_c)
    aT = dtT * (-jnp.exp(alog_c))
    li = lax.broadcasted_iota(jnp.int32, (L, L), 0)
    si = lax.broadcasted_iota(jnp.int32, (L, L), 1)
    incl = (si >= li) if reverse else (si <= li)
    inclT = (li >= si) if reverse else (li <= si)
    tri = jnp.where(incl, 1.0, 0.0).astype(BF16)
    triT = jnp.where(inclT, 1.0, 0.0).astype(BF16)
    p = sum(jnp.dot(tri, part, preferred_element_type=F32) for part in _split_bf16(a, 3))
    pT = sum(jnp.dot(part, triT, preferred_element_type=F32) for part in _split_bf16(aT, 3))
    edge = 0 if reverse else L - 1
    tot = p[edge:edge + 1, :]
    dec = jnp.exp(tot - p)
    pe = jnp.exp(p)

    hh = lax.broadcasted_iota(jnp.int32, (2 * SSD_HEADS, SSD_HEADS * SSD_HEAD_DIM), 0) % SSD_HEADS
    jj = lax.broadcasted_iota(jnp.int32, (2 * SSD_HEADS, SSD_HEADS * SSD_HEAD_DIM), 1) // SSD_HEAD_DIM
    expand_m = jnp.where(hh == jj, 1.0, 0.0).astype(BF16)

    def expand(v):
        hi, lo = _split_bf16(v, 2)
        return jnp.dot(jnp.concatenate([hi, lo], axis=1), expand_m, preferred_element_type=F32)

    dt_x = expand(dt)
    dtdec_x = expand(dt * dec)
    pe_x = expand(pe)
    X = (xs * dt_x).astype(BF16)
    Xs = (xs * dtdec_x).astype(BF16)
    lane = lax.broadcasted_iota(jnp.int32, (L, 2 * SSD_HEAD_DIM), 1)
    ns = SSD_GROUPS * SSD_STATE
    y_groups = []
    for g in range(SSD_GROUPS):
        Bg = bc[:, g * SSD_STATE:(g + 1) * SSD_STATE].astype(BF16)
        Cg = bc[:, ns + g * SSD_STATE:ns + (g + 1) * SSD_STATE].astype(BF16)
        cb = lax.dot_general(Cg, Bg, (((1,), (1,)), ((), ())), preferred_element_type=F32)
        ys = []
        for pr in range(SSD_HEADS // SSD_GROUPS // 2):
            h0 = g * (SSD_HEADS // SSD_GROUPS) + 2 * pr
            ms = []
            for h in (h0, h0 + 1):
                diff = p[:, h:h + 1] - pT[h:h + 1, :]
                ms.append(cb * jnp.exp(jnp.where(incl, diff, -jnp.inf)))
            lhs = jnp.concatenate(ms, axis=1).astype(BF16)
            xp = X[:, h0 * SSD_HEAD_DIM:(h0 + 2) * SSD_HEAD_DIM]
            zero = jnp.zeros_like(xp)
            rhs = jnp.concatenate([jnp.where(lane < SSD_HEAD_DIM, xp, zero),
                                   jnp.where(lane >= SSD_HEAD_DIM, xp, zero)], axis=0)
            ys.append(jnp.dot(lhs, rhs, preferred_element_type=F32))
        y_diag = jnp.concatenate(ys, axis=1)
        h_old = h_ref[g]
        y_off = jnp.dot(Cg, h_old.astype(BF16), preferred_element_type=F32) * pe_x[:, g * hp:(g + 1) * hp]
        st = lax.dot_general(Bg, Xs[:, g * hp:(g + 1) * hp], (((0,), (0,)), ((), ())),
                             preferred_element_type=F32)
        h_ref[g] = h_old * pe_x[edge:edge + 1, g * hp:(g + 1) * hp] + st
        y_groups.append(y_diag + y_off)
    return jnp.concatenate(y_groups, axis=1)


def _ssd_kernel(xsf_ref, xsb_ref, bcf_ref, bcb_ref, dtf_ref, dtb_ref, dtTf_ref, dtTb_ref,
                bias_r_ref, alog_r_ref, bias_c_ref, alog_c_ref, yf_ref, yb_ref, h_scr):
    @pl.when(pl.program_id(1) == 0)
    def _():
        h_scr[...] = jnp.zeros_like(h_scr)

    H = SSD_HEADS
    yf_ref[...] = _ssd_direction(
        xsf_ref[...], bcf_ref[...], dtf_ref[:, 0:H], dtTf_ref[0:H, :],
        bias_r_ref[:, 0:H], alog_r_ref[:, 0:H], bias_c_ref[0:H, :], alog_c_ref[0:H, :],
        h_scr.at[0], reverse=False)
    yb_ref[...] = _ssd_direction(
        xsb_ref[...], bcb_ref[...], dtb_ref[:, H:2 * H], dtTb_ref[H:2 * H, :],
        bias_r_ref[:, H:2 * H], alog_r_ref[:, H:2 * H], bias_c_ref[H:2 * H, :], alog_c_ref[H:2 * H, :],
        h_scr.at[1], reverse=True)


def _ssd_scan(xs, bc, proj3, dt_bias, a_log):
    B, S, d_inner = xs.shape
    L = SSD_CHUNK
    nc = S // L
    H2 = 2 * SSD_HEADS
    dtT = jnp.swapaxes(proj3[:, :, COL_DT:COL_DT + H2], 1, 2)
    fwd = lambda b, c: (b, c, 0)
    bwd = lambda b, c: (b, nc - 1 - c, 0)
    const = lambda shape: pl.BlockSpec(shape, lambda b, c: (0,) * len(shape))
    hp = SSD_HEADS // SSD_GROUPS * SSD_HEAD_DIM
    return pl.pallas_call(
        _ssd_kernel,
        out_shape=(jax.ShapeDtypeStruct((B, S, d_inner), F32),
                   jax.ShapeDtypeStruct((B, S, d_inner), F32)),
        grid=(B, nc),
        in_specs=[pl.BlockSpec((None, L, d_inner), fwd),
                  pl.BlockSpec((None, L, d_inner), bwd),
                  pl.BlockSpec((None, L, bc.shape[-1]), fwd),
                  pl.BlockSpec((None, L, bc.shape[-1]), bwd),
                  pl.BlockSpec((None, L, 128), lambda b, c: (b, c, COL_DT // 128)),
                  pl.BlockSpec((None, L, 128), lambda b, c: (b, nc - 1 - c, COL_DT // 128)),
                  pl.BlockSpec((None, H2, L), lambda b, c: (b, 0, c)),
                  pl.BlockSpec((None, H2, L), lambda b, c: (b, 0, nc - 1 - c)),
                  const((1, H2)), const((1, H2)), const((H2, 1)), const((H2, 1))],
        out_specs=(pl.BlockSpec((None, L, d_inner), fwd),
                   pl.BlockSpec((None, L, d_inner), bwd)),
        scratch_shapes=[pltpu.VMEM((2, SSD_GROUPS, SSD_STATE, hp), F32)],
        compiler_params=_cparams("parallel", "arbitrary"),
        name="ssd_scan",
    )(xs, xs, bc, bc, proj3, proj3, dtT, dtT,
      dt_bias.reshape(1, H2), a_log.reshape(1, H2), dt_bias.reshape(H2, 1), a_log.reshape(H2, 1))


def _merge_kernel(yf_ref, yb_ref, xs_ref, z_ref, ga_ref, gs_ref, att_ref, x_ref,
                  dskip_ref, snw_ref, wa_ref, ws_ref, wo_ref, g1_ref,
                  n2w_ref, sh2_ref, sc2_ref, wq_ref,
                  x1_ref, h2_ref, qp_ref):
    y = yf_ref[...] + yb_ref[...] + xs_ref[...] * dskip_ref[...]
    z = z_ref[...]
    ssd = _rms_scale(y * (z * _sigmoid(z))) * snw_ref[...]
    up_a = jnp.dot(att_ref[...].astype(BF16), wa_ref[...], preferred_element_type=F32)
    up_s = jnp.dot(ssd.astype(BF16), ws_ref[...], preferred_element_type=F32)
    merged = _sigmoid(ga_ref[...]) * up_a + _sigmoid(gs_ref[...]) * up_s
    mix = jnp.dot(merged.astype(BF16), wo_ref[...], preferred_element_type=F32)
    x1 = x_ref[...] + g1_ref[0] * mix
    x1_ref[...] = x1
    h2 = (_rms_scale(x1) * n2w_ref[...]) * (1.0 + sc2_ref[0]) + sh2_ref[0]
    h2b = h2.astype(BF16)
    h2_ref[...] = h2b
    qp_ref[...] = jnp.dot(h2b, wq_ref[...], preferred_element_type=F32)


def _merge(yf, yb, xs, proj, att, x2, d_skip_x, ssd_norm_w, wa, ws, wo, g1, norm2_w, sh2, sc2, wq, S):
    N, D = x2.shape
    tm = min(256, S)
    per_b = S // tm
    aw = att.shape[1]
    QW = wq.shape[1]
    row = lambda w, c=0: pl.BlockSpec((tm, w), lambda i: (i, c))
    const = lambda shape: pl.BlockSpec(shape, lambda i: (0,) * len(shape))
    perb = pl.BlockSpec((1, 1, D), lambda i: (i // per_b, 0, 0))
    return pl.pallas_call(
        _merge_kernel,
        out_shape=(jax.ShapeDtypeStruct((N, D), F32),
                   jax.ShapeDtypeStruct((N, D), BF16),
                   jax.ShapeDtypeStruct((N, QW), F32)),
        grid=(N // tm,),
        in_specs=[row(D), row(D), row(D),
                  row(D, COL_Z // D), row(D, COL_GATE // D), row(D, COL_GATE // D + 1),
                  row(aw), row(D),
                  const((1, D)), const((1, D)),
                  const((aw, D)), const((D, D)), const((D, D)), perb,
                  const((1, D)), perb, perb, const((D, QW))],
        out_specs=(row(D), row(D), row(QW)),
        compiler_params=_cparams("parallel"),
        name="merge",
    )(yf, yb, xs, proj, proj, proj, att, x2,
      d_skip_x, ssd_norm_w.reshape(1, D), wa, ws, wo, g1,
      norm2_w.reshape(1, D), sh2, sc2, wq)


def _top_values(s, k):
    vals = []
    for _ in range(k):
        m = jnp.max(s, axis=0, keepdims=True)
        vals.append(m)
        s = jnp.where(s >= m, -jnp.inf, s)
    return vals


def _peer_route_kernel(q_ref, k1h_ref, k1l_ref, k2h_ref, k2l_ref, th_ref, e1z_ref, s2_ref, e2_ref):
    K = PEER_TOPK
    dh = k1h_ref.shape[-1]
    q = q_ref[...]
    nt = (((1,), (1,)), ((), ()))

    def scores_t(qf, kh, kl):
        qh, ql = _split_bf16(qf, 2)
        dg = functools.partial(lax.dot_general, dimension_numbers=nt, preferred_element_type=F32)
        return dg(kh, qh) + dg(kl, qh) + dg(kh, ql)

    s1 = scores_t(q[:, :dh], k1h_ref[...], k1l_ref[...])
    s2 = scores_t(q[:, dh:], k2h_ref[...], k2l_ref[...])
    v1 = _top_values(s1, K)
    v2 = _top_values(s2, K)
    tt = s1.shape[1]
    inf = jnp.inf
    v2_all = jnp.concatenate(v2, axis=0)
    v2_8 = jnp.concatenate(v2[:8], axis=0)
    v1_hi = jnp.concatenate(v1[8:], axis=0)
    rowid = lax.broadcasted_iota(jnp.int32, (8, tt), 0)
    pieces = [v1[0] + v2_all]
    for a in range(1, 8):
        pieces.append(jnp.where(rowid < K // (a + 1), v1[a] + v2_8, -inf))
    pieces.append(v1_hi + v2[0])
    cand = jnp.concatenate(pieces, axis=0)
    tau = _top_values(cand, K)[-1]
    top = v1[0] + v2[0]
    zsum = jnp.sum(jnp.where(cand >= tau, jnp.exp(cand - top), 0.0), axis=0, keepdims=True)
    th_rank = [jnp.min(jnp.where(pieces[0] >= tau, v2_all, inf), axis=0, keepdims=True)]
    for a in range(1, 8):
        th_rank.append(jnp.min(jnp.where(pieces[a] >= tau, v2_8, inf), axis=0, keepdims=True))
    th_hi = jnp.where(pieces[8] >= tau, v2[0], inf)
    for a in range(8, K):
        th_rank.append(th_hi[a - 8:a - 7, :])
    theta = jnp.full(s1.shape, inf, F32)
    for a in range(K):
        theta = jnp.where(s1 == v1[a], th_rank[a], theta)
    th_ref[...] = theta
    e1z_ref[...] = jnp.exp(s1 - v1[0]) * (1.0 / zsum)
    s2_ref[...] = s2
    e2_ref[...] = jnp.exp(s2 - v2[0])


def _peer_route(qp, keys1, keys2):
    N = qp.shape[0]
    Hp, nk, dh = keys1.shape
    tt = 256
    k1h, k1l = _split_bf16(keys1, 2)
    k2h, k2l = _split_bf16(keys2, 2)
    kspec = pl.BlockSpec((None, nk, dh), lambda i, h: (h, 0, 0))
    ospec = pl.BlockSpec((None, nk, tt), lambda i, h: (h, 0, i))
    oshape = jax.ShapeDtypeStruct((Hp, nk, N), F32)
    return pl.pallas_call(
        _peer_route_kernel,
        out_shape=(oshape,) * 4,
        grid=(N // tt, Hp),
        in_specs=[pl.BlockSpec((tt, 2 * dh), lambda i, h: (i, h)), kspec, kspec, kspec, kspec],
        out_specs=(ospec,) * 4,
        compiler_params=_cparams("parallel", "arbitrary"),
        name="peer_route",
    )(qp, k1h, k1l, k2h, k2l)


def _gelu_tanh(x):
    return 0.5 * x * (1.0 + jnp.tanh(math.sqrt(2.0 / math.pi) * (x + 0.044715 * (x * x * x))))


def _peer_dense_kernel(ht_ref, u_ref, vt_ref, th_ref, e1z_ref, s2_ref, e2_ref, y_ref,
                       a_scr, wg_scr, *, n_slab):
    nk = PEER_KEYS
    T = ht_ref.shape[1]

    @pl.when(pl.program_id(1) == 0)
    def _():
        y_ref[...] = jnp.zeros_like(y_ref)

    a_scr[...] = jnp.dot(u_ref[...], ht_ref[...], preferred_element_type=F32)

    def col_tile(c, carry):
        cols = pl.ds(pl.multiple_of(c * 128, 128), 128)
        for k in range(n_slab):
            rows = slice(k * nk, (k + 1) * nk)
            w = jnp.zeros((nk, 128), F32)
            for h in range(PEER_HEADS):
                th = th_ref[h, k:k + 1, cols]
                ez = e1z_ref[h, k:k + 1, cols]
                w = w + jnp.where(s2_ref[h, :, cols] >= th, e2_ref[h, :, cols], 0.0) * ez
            act = _gelu_tanh(a_scr[rows, cols])
            wg_scr[rows, cols] = (w * act).astype(BF16)
        return carry

    lax.fori_loop(0, T // 128, col_tile, 0)
    y_ref[...] += jnp.dot(vt_ref[...], wg_scr[...], preferred_element_type=F32)


def _peer_dense(h2t, u_bf16, vt_bf16, th, e1z, s2, e2):
    D, N = h2t.shape
    E = u_bf16.shape[0]
    Hp, nk, _ = th.shape
    T = min(512, N)
    Ec = 1024
    n_slab = Ec // nk
    res = pl.BlockSpec((Hp, nk, T), lambda i, j: (0, 0, i))
    rows = pl.BlockSpec((Hp, n_slab, T), lambda i, j: (0, j, i))
    return pl.pallas_call(
        functools.partial(_peer_dense_kernel, n_slab=n_slab),
        out_shape=jax.ShapeDtypeStruct((D, N), F32),
        grid=(N // T, E // Ec),
        in_specs=[pl.BlockSpec((D, T), lambda i, j: (0, i)),
                  pl.BlockSpec((Ec, D), lambda i, j: (j, 0)),
                  pl.BlockSpec((D, Ec), lambda i, j: (0, j)),
                  rows, rows, res, res],
        out_specs=pl.BlockSpec((D, T), lambda i, j: (0, i)),
        scratch_shapes=[pltpu.VMEM((Ec, T), F32), pltpu.VMEM((Ec, T), BF16)],
        compiler_params=_cparams("parallel", "arbitrary"),
        name="peer_dense",
    )(h2t, u_bf16, vt_bf16, th, e1z, s2, e2)


def _final_kernel(x1_ref, pe_ref, g2_ref, w_ref, o_ref):
    o_ref[...] = _rms_scale(x1_ref[...] + g2_ref[0] * pe_ref[...]) * w_ref[...]


def _final(x1, peer_out, g2, final_w, S):
    N, D = x1.shape
    tm = min(512, S)
    per_b = S // tm
    row = pl.BlockSpec((tm, D), lambda i: (i, 0))
    return pl.pallas_call(
        _final_kernel,
        out_shape=jax.ShapeDtypeStruct((N, D), F32),
        grid=(N // tm,),
        in_specs=[row, row, pl.BlockSpec((1, 1, D), lambda i: (i // per_b, 0, 0)),
                  pl.BlockSpec((1, D), lambda i: (0, 0))],
        out_specs=row,
        compiler_params=_cparams("parallel"),
        name="final_norm",
    )(x1, peer_out, g2, final_w.reshape(1, D))


def _reorder_w_in(w_in):
    qw = N_HEADS * HEAD_DIM
    kw = N_KV_HEADS * HEAD_DIM
    d_inner = SSD_HEADS * SSD_HEAD_DIM
    xbc_w = d_inner + 2 * SSD_GROUPS * SSD_STATE
    o1 = qw
    o2 = o1 + kw
    o3 = o2 + kw
    o4 = o3 + d_inner
    o5 = o4 + xbc_w
    o6 = o5 + 2 * SSD_HEADS
    w = jnp.concatenate([w_in[:, o3:o4], w_in[:, o6:], w_in[:, o4:o5], w_in[:, :o3], w_in[:, o5:o6]], axis=1)
    return jnp.pad(w, ((0, 0), (0, PROJ_W - w.shape[1]))).astype(BF16)


def kernel(x, c, ada_w, ada_b, norm1_w, w_in, q_gain, k_gain, conv_w, conv_b, dt_bias_f, dt_bias_b,
           a_log_f, a_log_b, d_skip, ssd_norm_w, w_attn_up, w_ssd_up, w_out, norm2_w, peer_w_query,
           peer_keys1, peer_keys2, peer_u, peer_v, final_norm_w):
    B, S, D = x.shape
    assert ada_w.shape[0] == 1, "single-layer block"
    N = B * S
    x2 = x.reshape(N, D)
    mod = _adaln(c, ada_w[0], ada_b[0]).reshape(B, 6, 1, D)
    sh1, sc1, g1, sh2, sc2, g2 = (mod[:, k] for k in range(6))
    proj = _in_proj(x2, norm1_w[0], sh1, sc1, _reorder_w_in(w_in[0]), S)
    proj3 = proj.reshape(B, S, PROJ_W)
    att = _attention(proj3, q_gain[0], k_gain[0])
    xs, bc = _conv_silu(proj3, conv_w[0], conv_b[0])
    yf, yb = _ssd_scan(xs, bc, proj3,
                       jnp.concatenate([dt_bias_f[0], dt_bias_b[0]]),
                       jnp.concatenate([a_log_f[0], a_log_b[0]]))
    x1, h2, qp = _merge(
        yf.reshape(N, -1), yb.reshape(N, -1), xs.reshape(N, -1), proj, att.reshape(N, -1), x2,
        jnp.repeat(d_skip[0], SSD_HEAD_DIM).reshape(1, -1), ssd_norm_w[0],
        w_attn_up[0].astype(BF16), w_ssd_up[0].astype(BF16), w_out[0].astype(BF16), g1,
        norm2_w[0], sh2, sc2, peer_w_query[0].astype(BF16), S)
    th, e1z, s2, e2 = _peer_route(qp, peer_keys1[0], peer_keys2[0])
    yt = _peer_dense(h2.T, peer_u[0].astype(BF16), peer_v[0].T.astype(BF16), th, e1z, s2, e2)
    out = _final(x1, yt.T, g2, final_norm_w, S)
    return out.reshape(B, S, D)
```

```python
import functools
import math

import jax
import jax.numpy as jnp
from jax import lax
from jax.experimental import pallas as pl
from jax.experimental.pallas import tpu as pltpu

F32 = jnp.float32
BF16 = jnp.bfloat16

EPS = 1e-6
GRID_W = 64
ROPE_THETA = 10000.0
N_HEADS = 8
N_KV_HEADS = 2
HEAD_DIM = 64
SSD_HEADS = 16
SSD_HEAD_DIM = 64
SSD_GROUPS = 2
SSD_STATE = 128
SSD_CHUNK = 128
CONV_W = 5
PEER_HEADS = 8
PEER_KEYS = 128
PEER_TOPK = 16
D_MODEL = 1024

VMEM_LIMIT_BYTES = 56 * 1024 * 1024
BF16_SUBLANES = 16

COL_Z = 0
COL_GATE = 1024
COL_XBC = 3072
COL_Q = 4608
COL_K = 5120
COL_V = 5248
COL_DT = 5376
PROJ_W = 5632


def _cparams(*sem):
    return pltpu.CompilerParams(dimension_semantics=sem, vmem_limit_bytes=VMEM_LIMIT_BYTES)


def _split_bf16(x, n):
    parts = []
    r = x
    for _ in range(n):
        p = r.astype(BF16)
        parts.append(p)
        r = r - p.astype(F32)
    return parts


def _sigmoid(x):
    return 1.0 / (1.0 + jnp.exp(-x))


def _softplus(x):
    return jnp.maximum(x, 0.0) + jnp.log1p(jnp.exp(-jnp.abs(x)))


def _rms_scale(x):
    return x * lax.rsqrt(jnp.mean(x * x, axis=-1, keepdims=True) + EPS)


def _adaln_kernel(c_ref, w_ref, b_ref, o_ref):
    c = c_ref[...]
    act = c * _sigmoid(c)
    o_ref[...] = jnp.dot(act, w_ref[...], preferred_element_type=F32,
                         precision=lax.Precision.HIGHEST) + b_ref[...]


def _adaln(c, ada_w, ada_b):
    B, D = c.shape
    N = ada_w.shape[1]
    tn = 512
    return pl.pallas_call(
        _adaln_kernel,
        out_shape=jax.ShapeDtypeStruct((B, N), F32),
        grid=(N // tn,),
        in_specs=[pl.BlockSpec((B, D), lambda j: (0, 0)),
                  pl.BlockSpec((D, tn), lambda j: (0, j)),
                  pl.BlockSpec((1, tn), lambda j: (0, j))],
        out_specs=pl.BlockSpec((B, tn), lambda j: (0, j)),
        compiler_params=_cparams("arbitrary"),
        name="adaln",
    )(c, ada_w, ada_b.reshape(1, N))


def _in_proj_kernel(x_ref, nw_ref, sh_ref, sc_ref, w_ref, o_ref, h_scr):
    @pl.when(pl.program_id(1) == 0)
    def _():
        y = _rms_scale(x_ref[...]) * nw_ref[...]
        h_scr[...] = (y * (1.0 + sc_ref[0]) + sh_ref[0]).astype(BF16)

    o_ref[...] = jnp.dot(h_scr[...], w_ref[...], preferred_element_type=F32)


def _in_proj(x2, norm_w, shift, scale, w_bf16, S):
    N, D = x2.shape
    W = w_bf16.shape[1]
    tm = min(1024, S)
    tn = 512
    per_b = S // tm
    return pl.pallas_call(
        _in_proj_kernel,
        out_shape=jax.ShapeDtypeStruct((N, W), F32),
        grid=(N // tm, W // tn),
        in_specs=[pl.BlockSpec((tm, D), lambda i, j: (i, 0)),
                  pl.BlockSpec((1, D), lambda i, j: (0, 0)),
                  pl.BlockSpec((1, 1, D), lambda i, j: (i // per_b, 0, 0)),
                  pl.BlockSpec((1, 1, D), lambda i, j: (i // per_b, 0, 0)),
                  pl.BlockSpec((D, tn), lambda i, j: (0, j))],
        out_specs=pl.BlockSpec((tm, tn), lambda i, j: (i, j)),
        scratch_shapes=[pltpu.VMEM((tm, D), BF16)],
        compiler_params=_cparams("parallel", "arbitrary"),
        name="in_proj",
    )(x2, norm_w.reshape(1, D), shift, scale, w_bf16)


def _group_meansq(x, gmat):
    hi, lo = _split_bf16(x * x, 2)
    return (jnp.dot(hi, gmat, preferred_element_type=F32)
            + jnp.dot(lo, gmat, preferred_element_type=F32))


def _rope(x, cos, sin_a, sin_b):
    n = x.shape[1]
    return x * cos + pltpu.roll(x, n - 16, 1) * sin_a + pltpu.roll(x, 16, 1) * sin_b


LOG2E = 1.4426950408889634
SHIFT_MARGIN = 1.02
SHIFT_LIMIT = 60.0


def _attn_kernel(q_ref, k_ref, v_ref, cos_ref, sa_ref, sb_ref, gq_ref, gk_ref,
                 qg_ref, kg_ref, o_ref, kt_scr, v_scr, kmax_scr, *, tq):
    i = pl.program_id(1)
    n_q = N_HEADS // N_KV_HEADS
    hd = HEAD_DIM

    @pl.when(i == 0)
    def _():
        k = k_ref[...]
        S = k.shape[0]
        kn = k * lax.rsqrt(_group_meansq(k, gk_ref[...]) + EPS) * kg_ref[...]
        kr = _rope(kn, cos_ref[...], sa_ref[...], sb_ref[...])
        k2max = jnp.max(_group_meansq(kr, gk_ref[...]) * hd, axis=0, keepdims=True)
        per_q = jnp.concatenate([k2max[:, g * hd:(g + 1) * hd] for g in range(N_KV_HEADS) for _ in range(n_q)],
                                axis=1)
        kmax_scr[...] = jnp.broadcast_to(jnp.sqrt(per_q), kmax_scr.shape)
        krt = kr.T
        one_row = jnp.where(lax.broadcasted_iota(jnp.int32, (hd, S), 0) == 0, 1.0, 0.0)
        v = v_ref[...]
        ones = jnp.ones((S, hd), F32)
        for g in range(N_KV_HEADS):
            kt_scr[g] = jnp.concatenate([krt[g * hd:(g + 1) * hd, :], one_row], axis=0).astype(BF16)
            v_scr[g] = jnp.concatenate([v[:, g * hd:(g + 1) * hd], ones], axis=1).astype(BF16)

    rows = pl.ds(pl.multiple_of(i * tq, tq), tq)
    tile = lambda t: jnp.concatenate([t] * (N_HEADS // 2), axis=1)
    q = q_ref[...]
    qn = q * lax.rsqrt(_group_meansq(q, gq_ref[...]) + EPS) * (qg_ref[...] * (hd ** -0.5 * LOG2E))
    qr = _rope(qn, tile(cos_ref[rows, :]), tile(sa_ref[rows, :]), tile(sb_ref[rows, :]))
    qnorm = jnp.sqrt(_group_meansq(qr, gq_ref[...]) * hd)
    bound = qnorm * kmax_scr[0:1, :] * SHIFT_MARGIN
    safe = jnp.max(bound) <= SHIFT_LIMIT
    qb = qr.astype(BF16)
    head_lane0 = jnp.bitwise_and(lax.broadcasted_iota(jnp.int32, bound.shape, 1), hd - 1) == 0
    nb = jnp.where(head_lane0, -bound, 0.0).astype(BF16)
    qs = []
    for g in range(N_KV_HEADS):
        heads = range(n_q * g, n_q * (g + 1))
        qs.append(jnp.concatenate(
            [jnp.concatenate([qb[:, h * hd:(h + 1) * hd], nb[:, h * hd:(h + 1) * hd]], axis=1) for h in heads],
            axis=0))

    def finish(acc):
        o = acc[:, :hd] / acc[:, hd:]
        return [o[r * tq:(r + 1) * tq, :] for r in range(n_q)]

    @pl.when(safe)
    def _():
        outs = []
        for g in range(N_KV_HEADS):
            s = jnp.dot(qs[g], kt_scr[g], preferred_element_type=F32)
            outs += finish(jnp.dot(jnp.exp2(s).astype(BF16), v_scr[g], preferred_element_type=F32))
        o_ref[...] = jnp.concatenate(outs, axis=1)

    @pl.when(jnp.logical_not(safe))
    def _():
        outs = []
        for g in range(N_KV_HEADS):
            s = jnp.dot(qs[g], kt_scr[g], preferred_element_type=F32)
            p = jnp.exp2(s - jnp.max(s, axis=-1, keepdims=True))
            outs += finish(jnp.dot(p.astype(BF16), v_scr[g], preferred_element_type=F32))
        o_ref[...] = jnp.concatenate(outs, axis=1)


def _rope_tables(S):
    half = HEAD_DIM // 2
    inv = ROPE_THETA ** (-jnp.arange(0, half, 2, dtype=F32) / half)
    t = jnp.arange(S, dtype=jnp.int32)
    ang_r = (t // GRID_W).astype(F32)[:, None] * inv
    ang_c = (t % GRID_W).astype(F32)[:, None] * inv
    zero = jnp.zeros_like(ang_r)
    cos = jnp.concatenate([jnp.cos(ang_r)] * 2 + [jnp.cos(ang_c)] * 2, axis=1)
    sin_a = jnp.concatenate([-jnp.sin(ang_r), zero, -jnp.sin(ang_c), zero], axis=1)
    sin_b = jnp.concatenate([zero, jnp.sin(ang_r), zero, jnp.sin(ang_c)], axis=1)
    two = lambda a: jnp.concatenate([a, a], axis=1)
    return two(cos), two(sin_a), two(sin_b)


def _group_matrix(n, group):
    idx = jnp.arange(n) // group
    return jnp.where(idx[:, None] == idx[None, :], 1.0 / group, 0.0).astype(BF16)


def _attention(proj3, q_gain, k_gain):
    B, S, _ = proj3.shape
    tq = 128
    cos, sin_a, sin_b = _rope_tables(S)
    qw = N_HEADS * HEAD_DIM
    kw = N_KV_HEADS * HEAD_DIM
    const = lambda shape: pl.BlockSpec(shape, lambda b, i: (0,) * len(shape))
    return pl.pallas_call(
        functools.partial(_attn_kernel, tq=tq),
        out_shape=jax.ShapeDtypeStruct((B, S, qw), F32),
        grid=(B, S // tq),
        in_specs=[pl.BlockSpec((None, tq, qw), lambda b, i: (b, i, COL_Q // qw)),
                  pl.BlockSpec((None, S, kw), lambda b, i: (b, 0, COL_K // kw)),
                  pl.BlockSpec((None, S, kw), lambda b, i: (b, 0, COL_V // kw)),
                  const((S, kw)), const((S, kw)), const((S, kw)),
                  const((qw, qw)), const((kw, kw)), const((1, qw)), const((1, kw))],
        out_specs=pl.BlockSpec((None, tq, qw), lambda b, i: (b, i, 0)),
        scratch_shapes=[pltpu.VMEM((N_KV_HEADS, 2 * HEAD_DIM, S), BF16),
                        pltpu.VMEM((N_KV_HEADS, S, 2 * HEAD_DIM), BF16),
                        pltpu.VMEM((8, qw), F32)],
        compiler_params=_cparams("parallel", "arbitrary"),
        name="attention",
    )(proj3, proj3, proj3, cos, sin_a, sin_b,
      _group_matrix(qw, HEAD_DIM), _group_matrix(kw, HEAD_DIM),
      jnp.tile(q_gain, N_HEADS).reshape(1, qw), jnp.tile(k_gain, N_KV_HEADS).reshape(1, kw))


def _conv_kernel(prev_ref, main_ref, next_ref, cw_ref, cb_ref, xs_ref, bc_ref, ext_scr, *, ts):
    i = pl.program_id(1)
    last = pl.num_programs(1) - 1
    ext_scr[0:8, :] = jnp.where(i > 0, prev_ref[...], 0.0)
    ext_scr[8:8 + ts, :] = main_ref[...]
    ext_scr[8 + ts:16 + ts, :] = jnp.where(i < last, next_ref[...], 0.0)
    pad = (CONV_W - 1) // 2
    acc = cb_ref[...] + ext_scr[pl.ds(8 - pad, ts), :] * cw_ref[0:1, :]
    for w in range(1, CONV_W):
        acc = acc + ext_scr[pl.ds(8 - pad + w, ts), :] * cw_ref[w:w + 1, :]
    u = acc * _sigmoid(acc)
    d_inner = xs_ref.shape[-1]
    xs_ref[...] = u[:, :d_inner]
    bc_ref[...] = u[:, d_inner:]


def _conv_silu(proj3, conv_w, conv_b):
    B, S, _ = proj3.shape
    C = conv_w.shape[1]
    d_inner = SSD_HEADS * SSD_HEAD_DIM
    ts = min(512, S)
    nb8 = S // 8
    cb = COL_XBC // C
    return pl.pallas_call(
        functools.partial(_conv_kernel, ts=ts),
        out_shape=(jax.ShapeDtypeStruct((B, S, d_inner), F32),
                   jax.ShapeDtypeStruct((B, S, C - d_inner), F32)),
        grid=(B, S // ts),
        in_specs=[pl.BlockSpec((None, 8, C), lambda b, i: (b, jnp.maximum(i * (ts // 8) - 1, 0), cb)),
                  pl.BlockSpec((None, ts, C), lambda b, i: (b, i, cb)),
                  pl.BlockSpec((None, 8, C), lambda b, i: (b, jnp.minimum((i + 1) * (ts // 8), nb8 - 1), cb)),
                  pl.BlockSpec((CONV_W, C), lambda b, i: (0, 0)),
                  pl.BlockSpec((1, C), lambda b, i: (0, 0))],
        out_specs=(pl.BlockSpec((None, ts, d_inner), lambda b, i: (b, i, 0)),
                   pl.BlockSpec((None, ts, C - d_inner), lambda b, i: (b, i, 0))),
        scratch_shapes=[pltpu.VMEM((ts + 16, C), F32)],
        compiler_params=_cparams("parallel", "arbitrary"),
        name="conv_silu",
    )(proj3, proj3, proj3, conv_w, conv_b.reshape(1, C))


def _ssd_direction(xs, bc, dt_raw, dtT_raw, bias_r, alog_r, bias_c, alog_c, h_ref, reverse):
    L = SSD_CHUNK
    hp = SSD_HEADS // SSD_GROUPS * SSD_HEAD_DIM
    dt = _softplus(dt_raw + bias_r)
    a = dt * (-jnp.exp(alog_r))
    dtT = _softplus(dtT_raw + bias_c)
    aT = dtT * (-jnp.exp(alog_c))
    li = lax.broadcasted_iota(jnp.int32, (L, L), 0)
    si = lax.broadcasted_iota(jnp.int32, (L, L), 1)
    incl = (si >= li) if reverse else (si <= li)
    inclT = (li >= si) if reverse else (li <= si)
    tri = jnp.where(incl, 1.0, 0.0).astype(BF16)
    triT = jnp.where(inclT, 1.0, 0.0).astype(BF16)
    p = sum(jnp.dot(tri, part, preferred_element_type=F32) for part in _split_bf16(a, 3))
    pT = sum(jnp.dot(part, triT, preferred_element_type=F32) for part in _split_bf16(aT, 3))
    edge = 0 if reverse else L - 1
    tot = p[edge:edge + 1, :]
    dec = jnp.exp(tot - p)
    pe = jnp.exp(p)

    hh = lax.broadcasted_iota(jnp.int32, (2 * SSD_HEADS, SSD_HEADS * SSD_HEAD_DIM), 0) % SSD_HEADS
    jj = lax.broadcasted_iota(jnp.int32, (2 * SSD_HEADS, SSD_HEADS * SSD_HEAD_DIM), 1) // SSD_HEAD_DIM
    expand_m = jnp.where(hh == jj, 1.0, 0.0).astype(BF16)

    def expand(v):
        hi, lo = _split_bf16(v, 2)
        return jnp.dot(jnp.concatenate([hi, lo], axis=1), expand_m, preferred_element_type=F32)

    dt_x = expand(dt)
    dtdec_x = expand(dt * dec)
    pe_x = expand(pe)
    X = (xs * dt_x).astype(BF16)
    Xs = (xs * dtdec_x).astype(BF16)
    lane = lax.broadcasted_iota(jnp.int32, (L, 2 * SSD_HEAD_DIM), 1)
    ns = SSD_GROUPS * SSD_STATE
    y_groups = []
    for g in range(SSD_GROUPS):
        Bg = bc[:, g * SSD_STATE:(g + 1) * SSD_STATE].astype(BF16)
        Cg = bc[:, ns + g * SSD_STATE:ns + (g + 1) * SSD_STATE].astype(BF16)
        cb = lax.dot_general(Cg, Bg, (((1,), (1,)), ((), ())), preferred_element_type=F32)
        ys = []
        for pr in range(SSD_HEADS // SSD_GROUPS // 2):
            h0 = g * (SSD_HEADS // SSD_GROUPS) + 2 * pr
            ms = []
            for h in (h0, h0 + 1):
                diff = p[:, h:h + 1] - pT[h:h + 1, :]
                ms.append(cb * jnp.exp(jnp.where(incl, diff, -jnp.inf)))
            lhs = jnp.concatenate(ms, axis=1).astype(BF16)
            xp = X[:, h0 * SSD_HEAD_DIM:(h0 + 2) * SSD_HEAD_DIM]
            zero = jnp.zeros_like(xp)
            rhs = jnp.concatenate([jnp.where(lane < SSD_HEAD_DIM, xp, zero),
                                   jnp.where(lane >= SSD_HEAD_DIM, xp, zero)], axis=0)
            ys.append(jnp.dot(lhs, rhs, preferred_element_type=F32))
        y_diag = jnp.concatenate(ys, axis=1)
        h_old = h_ref[g]
        y_off = jnp.dot(Cg, h_old.astype(BF16), preferred_element_type=F32) * pe_x[:, g * hp:(g + 1) * hp]
        st = lax.dot_general(Bg, Xs[:, g * hp:(g + 1) * hp], (((0,), (0,)), ((), ())),
                             preferred_element_type=F32)
        h_ref[g] = h_old * pe_x[edge:edge + 1, g * hp:(g + 1) * hp] + st
        y_groups.append(y_diag + y_off)
    return jnp.concatenate(y_groups, axis=1)


def _ssd_kernel(xsf_ref, xsb_ref, bcf_ref, bcb_ref, dtf_ref, dtb_ref, dtTf_ref, dtTb_ref,
                bias_r_ref, alog_r_ref, bias_c_ref, alog_c_ref, yf_ref, yb_ref, h_scr):
    @pl.when(pl.program_id(1) == 0)
    def _():
        h_scr[...] = jnp.zeros_like(h_scr)

    H = SSD_HEADS
    yf_ref[...] = _ssd_direction(
        xsf_ref[...], bcf_ref[...], dtf_ref[:, 0:H], dtTf_ref[0:H, :],
        bias_r_ref[:, 0:H], alog_r_ref[:, 0:H], bias_c_ref[0:H, :], alog_c_ref[0:H, :],
        h_scr.at[0], reverse=False)
    yb_ref[...] = _ssd_direction(
        xsb_ref[...], bcb_ref[...], dtb_ref[:, H:2 * H], dtTb_ref[H:2 * H, :],
        bias_r_ref[:, H:2 * H], alog_r_ref[:, H:2 * H], bias_c_ref[H:2 * H, :], alog_c_ref[H:2 * H, :],
        h_scr.at[1], reverse=True)


def _ssd_scan(xs, bc, proj3, dt_bias, a_log):
    B, S, d_inner = xs.shape
    L = SSD_CHUNK
    nc = S // L
    H2 = 2 * SSD_HEADS
    dtT = jnp.swapaxes(proj3[:, :, COL_DT:COL_DT + H2], 1, 2)
    fwd = lambda b, c: (b, c, 0)
    bwd = lambda b, c: (b, nc - 1 - c, 0)
    const = lambda shape: pl.BlockSpec(shape, lambda b, c: (0,) * len(shape))
    hp = SSD_HEADS // SSD_GROUPS * SSD_HEAD_DIM
    return pl.pallas_call(
        _ssd_kernel,
        out_shape=(jax.ShapeDtypeStruct((B, S, d_inner), F32),
                   jax.ShapeDtypeStruct((B, S, d_inner), F32)),
        grid=(B, nc),
        in_specs=[pl.BlockSpec((None, L, d_inner), fwd),
                  pl.BlockSpec((None, L, d_inner), bwd),
                  pl.BlockSpec((None, L, bc.shape[-1]), fwd),
                  pl.BlockSpec((None, L, bc.shape[-1]), bwd),
                  pl.BlockSpec((None, L, 128), lambda b, c: (b, c, COL_DT // 128)),
                  pl.BlockSpec((None, L, 128), lambda b, c: (b, nc - 1 - c, COL_DT // 128)),
                  pl.BlockSpec((None, H2, L), lambda b, c: (b, 0, c)),
                  pl.BlockSpec((None, H2, L), lambda b, c: (b, 0, nc - 1 - c)),
                  const((1, H2)), const((1, H2)), const((H2, 1)), const((H2, 1))],
        out_specs=(pl.BlockSpec((None, L, d_inner), fwd),
                   pl.BlockSpec((None, L, d_inner), bwd)),
        scratch_shapes=[pltpu.VMEM((2, SSD_GROUPS, SSD_STATE, hp), F32)],
        compiler_params=_cparams("parallel", "arbitrary"),
        name="ssd_scan",
    )(xs, xs, bc, bc, proj3, proj3, dtT, dtT,
      dt_bias.reshape(1, H2), a_log.reshape(1, H2), dt_bias.reshape(H2, 1), a_log.reshape(H2, 1))


def _merge_kernel(yf_ref, yb_ref, xs_ref, z_ref, ga_ref, gs_ref, att_ref, x_ref,
                  dskip_ref, snw_ref, wa_ref, ws_ref, wo_ref, g1_ref,
                  n2w_ref, sh2_ref, sc2_ref, wq_ref,
                  x1_ref, h2_ref, qp_ref):
    y = yf_ref[...] + yb_ref[...] + xs_ref[...] * dskip_ref[...]
    z = z_ref[...]
    ssd = _rms_scale(y * (z * _sigmoid(z))) * snw_ref[...]
    up_a = jnp.dot(att_ref[...].astype(BF16), wa_ref[...], preferred_element_type=F32)
    up_s = jnp.dot(ssd.astype(BF16), ws_ref[...], preferred_element_type=F32)
    merged = _sigmoid(ga_ref[...]) * up_a + _sigmoid(gs_ref[...]) * up_s
    mix = jnp.dot(merged.astype(BF16), wo_ref[...], preferred_element_type=F32)
    x1 = x_ref[...] + g1_ref[0] * mix
    x1_ref[...] = x1
    h2 = (_rms_scale(x1) * n2w_ref[...]) * (1.0 + sc2_ref[0]) + sh2_ref[0]
    h2b = h2.astype(BF16)
    h2_ref[...] = h2b
    qp_ref[...] = jnp.dot(h2b, wq_ref[...], preferred_element_type=F32)


def _merge(yf, yb, xs, proj, att, x2, d_skip_x, ssd_norm_w, wa, ws, wo, g1, norm2_w, sh2, sc2, wq, S):
    N, D = x2.shape
    tm = min(256, S)
    per_b = S // tm
    aw = att.shape[1]
    QW = wq.shape[1]
    row = lambda w, c=0: pl.BlockSpec((tm, w), lambda i: (i, c))
    const = lambda shape: pl.BlockSpec(shape, lambda i: (0,) * len(shape))
    perb = pl.BlockSpec((1, 1, D), lambda i: (i // per_b, 0, 0))
    return pl.pallas_call(
        _merge_kernel,
        out_shape=(jax.ShapeDtypeStruct((N, D), F32),
                   jax.ShapeDtypeStruct((N, D), BF16),
                   jax.ShapeDtypeStruct((N, QW), F32)),
        grid=(N // tm,),
        in_specs=[row(D), row(D), row(D),
                  row(D, COL_Z // D), row(D, COL_GATE // D), row(D, COL_GATE // D + 1),
                  row(aw), row(D),
                  const((1, D)), const((1, D)),
                  const((aw, D)), const((D, D)), const((D, D)), perb,
                  const((1, D)), perb, perb, const((D, QW))],
        out_specs=(row(D), row(D), row(QW)),
        compiler_params=_cparams("parallel"),
        name="merge",
    )(yf, yb, xs, proj, proj, proj, att, x2,
      d_skip_x, ssd_norm_w.reshape(1, D), wa, ws, wo, g1,
      norm2_w.reshape(1, D), sh2, sc2, wq)


def _top_values(s, k):
    vals = []
    for _ in range(k):
        m = jnp.max(s, axis=0, keepdims=True)
        vals.append(m)
        s = jnp.where(s >= m, -jnp.inf, s)
    return vals


def _peer_route_kernel(q_ref, k1h_ref, k1l_ref, k2h_ref, k2l_ref, n1_ref, e1z_ref, r2_ref, e2_ref):
    K = PEER_TOPK
    dh = k1h_ref.shape[-1]
    q = q_ref[...]
    nt = (((1,), (1,)), ((), ()))

    def scores_t(qf, kh, kl):
        qh, ql = _split_bf16(qf, 2)
        dg = functools.partial(lax.dot_general, dimension_numbers=nt, preferred_element_type=F32)
        return dg(kh, qh) + dg(kl, qh) + dg(kh, ql)

    s1 = scores_t(q[:, :dh], k1h_ref[...], k1l_ref[...])
    s2 = scores_t(q[:, dh:], k2h_ref[...], k2l_ref[...])
    v1 = _top_values(s1, K)
    v2 = _top_values(s2, K)
    tt = s1.shape[1]
    inf = jnp.inf
    v2_all = jnp.concatenate(v2, axis=0)
    v2_8 = jnp.concatenate(v2[:8], axis=0)
    v1_hi = jnp.concatenate(v1[8:], axis=0)
    rowid = lax.broadcasted_iota(jnp.int32, (8, tt), 0)
    pieces = [v1[0] + v2_all]
    for a in range(1, 8):
        pieces.append(jnp.where(rowid < K // (a + 1), v1[a] + v2_8, -inf))
    pieces.append(v1_hi + v2[0])
    cand = jnp.concatenate(pieces, axis=0)
    tau = _top_values(cand, K)[-1]
    top = v1[0] + v2[0]
    zsum = jnp.sum(jnp.where(cand >= tau, jnp.exp(cand - top), 0.0), axis=0, keepdims=True)
    count = lambda piece: jnp.sum(jnp.where(piece >= tau, 1.0, 0.0), axis=0, keepdims=True)
    n_rank = [count(pieces[a]) for a in range(8)]
    n_hi = jnp.where(pieces[8] >= tau, 1.0, 0.0)
    n_rank += [n_hi[a - 8:a - 7, :] for a in range(8, K)]
    n1 = jnp.zeros(s1.shape, F32)
    for a in range(K):
        n1 = jnp.where(s1 == v1[a], n_rank[a], n1)
    r2 = jnp.zeros(s2.shape, F32)
    for b in range(K):
        r2 = r2 + jnp.where(s2 < v2[b], 1.0, 0.0)
    n1_ref[...] = n1.astype(BF16)
    e1z_ref[...] = (jnp.exp(s1 - v1[0]) * (1.0 / zsum)).astype(BF16)
    r2_ref[...] = pltpu.bitcast(r2.astype(BF16), jnp.int32)
    e2_ref[...] = pltpu.bitcast(jnp.exp(s2 - v2[0]).astype(BF16), jnp.int32)


def _peer_route(qp, keys1, keys2):
    N = qp.shape[0]
    Hp, nk, dh = keys1.shape
    tt = 256
    k1h, k1l = _split_bf16(keys1, 2)
    k2h, k2l = _split_bf16(keys2, 2)
    kspec = pl.BlockSpec((None, nk, dh), lambda i, h: (h, 0, 0))
    ospec = pl.BlockSpec((None, nk, tt), lambda i, h: (h, 0, i))
    oshape = jax.ShapeDtypeStruct((Hp, nk, N), BF16)
    pspec = pl.BlockSpec((None, nk // 2, tt), lambda i, h: (h, 0, i))
    pshape = jax.ShapeDtypeStruct((Hp, nk // 2, N), jnp.int32)
    return pl.pallas_call(
        _peer_route_kernel,
        out_shape=(oshape, oshape, pshape, pshape),
        grid=(N // tt, Hp),
        in_specs=[pl.BlockSpec((tt, 2 * dh), lambda i, h: (i, h)), kspec, kspec, kspec, kspec],
        out_specs=(ospec, ospec, pspec, pspec),
        compiler_params=_cparams("parallel", "arbitrary"),
        name="peer_route",
    )(qp, k1h, k1l, k2h, k2l)


_GELU_C0 = -2.0 * math.sqrt(2.0 / math.pi) * LOG2E
_GELU_C1 = _GELU_C0 * 0.044715


def _gelu_tanh(x):
    return x / (1.0 + jnp.exp2(x * (x * x * _GELU_C1 + _GELU_C0)))


def _peer_weighted_act(a_ref, n1_ref, e1z_ref, r2_ref, e2_ref, wg_ref, slab0, k_range):
    nk = PEER_KEYS
    T = a_ref.shape[1]
    sub = BF16_SUBLANES
    zero = jnp.zeros((), BF16)
    for k in k_range:
        for c in range(T // 128):
            cols = slice(c * 128, (c + 1) * 128)
            w = [None] * (nk // sub)
            for h in range(PEER_HEADS):
                row = slice(slab0 + k, slab0 + k + 1)
                n = jnp.broadcast_to(n1_ref[h, :, cols].astype(F32)[row], (sub, 128)).astype(BF16)
                ez = jnp.broadcast_to(e1z_ref[h, :, cols].astype(F32)[row], (sub, 128)).astype(BF16)
                for j in range(nk // sub):
                    r2 = pltpu.bitcast(r2_ref[h, j], BF16)[:, cols]
                    e2 = pltpu.bitcast(e2_ref[h, j], BF16)[:, cols]
                    t = jnp.where(r2 < n, e2, zero) * ez
                    w[j] = t if w[j] is None else w[j] + t
            for j in range(nk // sub):
                rows = slice(k * nk + j * sub, k * nk + (j + 1) * sub)
                words = slice((k * nk + j * sub) // 2, (k * nk + (j + 1) * sub) // 2)
                wg_ref[words, cols] = pltpu.bitcast(w[j] * _gelu_tanh(a_ref[rows, cols]).astype(BF16), jnp.int32)


def _peer_dense_kernel(ht_ref, u_ref, vt_ref, n1p_ref, e1zp_ref, n1c_ref, e1zc_ref,
                       r2p_ref, e2p_ref, r2c_ref, e2c_ref, y_ref, a0, a1, wg0, wg1, *, blocks_per_tile):
    g = pl.program_id(0)
    half = a0.shape[0]
    slabs = half // PEER_KEYS

    @pl.when(g == 0)
    def _():
        a1[...] = jnp.zeros_like(a1)
        wg0[...] = jnp.zeros_like(wg0)

    @pl.when(jnp.maximum(g - 1, 0) % blocks_per_tile == 0)
    def _():
        y_ref[...] = jnp.zeros_like(y_ref)

    ht = ht_ref[...]
    a0[...] = jnp.dot(u_ref[0:half, :], ht, preferred_element_type=F32)
    _peer_weighted_act(a1, n1p_ref, e1zp_ref, r2p_ref, e2p_ref, wg1, slabs, range(slabs))
    y_ref[...] += jnp.dot(vt_ref[:, 0:half], pltpu.bitcast(wg0[...], BF16), preferred_element_type=F32)
    a1[...] = jnp.dot(u_ref[half:2 * half, :], ht, preferred_element_type=F32)
    _peer_weighted_act(a0, n1c_ref, e1zc_ref, r2c_ref, e2c_ref, wg0, 0, range(slabs))
    y_ref[...] += jnp.dot(vt_ref[:, half:2 * half], pltpu.bitcast(wg1[...], BF16), preferred_element_type=F32)


def _peer_dense(h2t, u_bf16, vt_bf16, n1, e1z, r2, e2):
    D, N = h2t.shape
    E = u_bf16.shape[0]
    Hp, nk, _ = n1.shape
    T = min(512, N)
    Eb = 1024
    slabs = Eb // nk
    n_t, n_b = N // T, E // Eb
    steps = n_t * n_b + 1
    tile = lambda g: jnp.minimum(g // n_b, n_t - 1)
    prev = lambda g: jnp.maximum(g - 1, 0)
    rows_p = pl.BlockSpec((Hp, None, slabs, T), lambda g: (0, prev(g) % n_b, 0, tile(prev(g))))
    rows_c = pl.BlockSpec((Hp, None, slabs, T), lambda g: (0, g % n_b, 0, tile(g)))
    sub = BF16_SUBLANES
    res_p = pl.BlockSpec((Hp, nk // sub, sub // 2, T), lambda g: (0, 0, 0, tile(prev(g))))
    res_c = pl.BlockSpec((Hp, nk // sub, sub // 2, T), lambda g: (0, 0, 0, tile(g)))
    n1r = n1.reshape(Hp, nk // slabs, slabs, N)
    e1zr = e1z.reshape(Hp, nk // slabs, slabs, N)
    r2 = r2.reshape(Hp, nk // sub, sub // 2, N)
    e2 = e2.reshape(Hp, nk // sub, sub // 2, N)
    return pl.pallas_call(
        functools.partial(_peer_dense_kernel, blocks_per_tile=n_b),
        out_shape=jax.ShapeDtypeStruct((D, N), F32),
        grid=(steps,),
        in_specs=[pl.BlockSpec((D, T), lambda g: (0, tile(g))),
                  pl.BlockSpec((Eb, D), lambda g: (g % n_b, 0)),
                  pl.BlockSpec((D, Eb), lambda g: (0, prev(g) % n_b)),
                  rows_p, rows_p, rows_c, rows_c, res_p, res_p, res_c, res_c],
        out_specs=pl.BlockSpec((D, T), lambda g: (0, tile(prev(g)))),
        scratch_shapes=[pltpu.VMEM((Eb // 2, T), F32), pltpu.VMEM((Eb // 2, T), F32),
                        pltpu.VMEM((Eb // 4, T), jnp.int32), pltpu.VMEM((Eb // 4, T), jnp.int32)],
        compiler_params=_cparams("arbitrary"),
        name="peer_dense",
    )(h2t, u_bf16, vt_bf16, n1r, e1zr, n1r, e1zr, r2, e2, r2, e2)


def _final_kernel(x1_ref, pe_ref, g2_ref, w_ref, o_ref):
    o_ref[...] = _rms_scale(x1_ref[...] + g2_ref[0] * pe_ref[...]) * w_ref[...]


def _final(x1, peer_out, g2, final_w, S):
    N, D = x1.shape
    tm = min(512, S)
    per_b = S // tm
    row = pl.BlockSpec((tm, D), lambda i: (i, 0))
    return pl.pallas_call(
        _final_kernel,
        out_shape=jax.ShapeDtypeStruct((N, D), F32),
        grid=(N // tm,),
        in_specs=[row, row, pl.BlockSpec((1, 1, D), lambda i: (i // per_b, 0, 0)),
                  pl.BlockSpec((1, D), lambda i: (0, 0))],
        out_specs=row,
        compiler_params=_cparams("parallel"),
        name="final_norm",
    )(x1, peer_out, g2, final_w.reshape(1, D))


def _reorder_w_in(w_in):
    qw = N_HEADS * HEAD_DIM
    kw = N_KV_HEADS * HEAD_DIM
    d_inner = SSD_HEADS * SSD_HEAD_DIM
    xbc_w = d_inner + 2 * SSD_GROUPS * SSD_STATE
    o1 = qw
    o2 = o1 + kw
    o3 = o2 + kw
    o4 = o3 + d_inner
    o5 = o4 + xbc_w
    o6 = o5 + 2 * SSD_HEADS
    w = jnp.concatenate([w_in[:, o3:o4], w_in[:, o6:], w_in[:, o4:o5], w_in[:, :o3], w_in[:, o5:o6]], axis=1)
    return jnp.pad(w, ((0, 0), (0, PROJ_W - w.shape[1]))).astype(BF16)


def kernel(x, c, ada_w, ada_b, norm1_w, w_in, q_gain, k_gain, conv_w, conv_b, dt_bias_f, dt_bias_b,
           a_log_f, a_log_b, d_skip, ssd_norm_w, w_attn_up, w_ssd_up, w_out, norm2_w, peer_w_query,
           peer_keys1, peer_keys2, peer_u, peer_v, final_norm_w):
    B, S, D = x.shape
    assert ada_w.shape[0] == 1, "single-layer block"
    N = B * S
    x2 = x.reshape(N, D)
    mod = _adaln(c, ada_w[0], ada_b[0]).reshape(B, 6, 1, D)
    sh1, sc1, g1, sh2, sc2, g2 = (mod[:, k] for k in range(6))
    proj = _in_proj(x2, norm1_w[0], sh1, sc1, _reorder_w_in(w_in[0]), S)
    proj3 = proj.reshape(B, S, PROJ_W)
    att = _attention(proj3, q_gain[0], k_gain[0])
    xs, bc = _conv_silu(proj3, conv_w[0], conv_b[0])
    yf, yb = _ssd_scan(xs, bc, proj3,
                       jnp.concatenate([dt_bias_f[0], dt_bias_b[0]]),
                       jnp.concatenate([a_log_f[0], a_log_b[0]]))
    x1, h2, qp = _merge(
        yf.reshape(N, -1), yb.reshape(N, -1), xs.reshape(N, -1), proj, att.reshape(N, -1), x2,
        jnp.repeat(d_skip[0], SSD_HEAD_DIM).reshape(1, -1), ssd_norm_w[0],
        w_attn_up[0].astype(BF16), w_ssd_up[0].astype(BF16), w_out[0].astype(BF16), g1,
        norm2_w[0], sh2, sc2, peer_w_query[0].astype(BF16), S)
    th, e1z, s2, e2 = _peer_route(qp, peer_keys1[0], peer_keys2[0])
    yt = _peer_dense(h2.T, peer_u[0].astype(BF16), peer_v[0].T.astype(BF16), th, e1z, s2, e2)
    out = _final(x1, yt.T, g2, final_norm_w, S)
    return out.reshape(B, S, D)
```

```python
import functools
import math

import jax
import jax.numpy as jnp
from jax import lax
from jax.experimental import pallas as pl
from jax.experimental.pallas import tpu as pltpu

F32 = jnp.float32
BF16 = jnp.bfloat16

EPS = 1e-6
GRID_W = 64
ROPE_THETA = 10000.0
N_HEADS = 8
N_KV_HEADS = 2
HEAD_DIM = 64
SSD_HEADS = 16
SSD_HEAD_DIM = 64
SSD_GROUPS = 2
SSD_STATE = 128
SSD_CHUNK = 128
CONV_W = 5
PEER_HEADS = 8
PEER_KEYS = 128
PEER_TOPK = 16
D_MODEL = 1024

VMEM_LIMIT_BYTES = 56 * 1024 * 1024
BF16_SUBLANES = 16

COL_Z = 0
COL_GATE = 1024
COL_XBC = 3072
COL_Q = 4608
COL_K = 5120
COL_V = 5248
COL_DT = 5376
PROJ_W = 5632


def _cparams(*sem):
    return pltpu.CompilerParams(dimension_semantics=sem, vmem_limit_bytes=VMEM_LIMIT_BYTES)


def _split_bf16(x, n):
    parts = []
    r = x
    for _ in range(n):
        p = r.astype(BF16)
        parts.append(p)
        r = r - p.astype(F32)
    return parts


def _sigmoid(x):
    return 1.0 / (1.0 + jnp.exp(-x))


def _softplus(x):
    return jnp.maximum(x, 0.0) + jnp.log1p(jnp.exp(-jnp.abs(x)))


def _rms_scale(x):
    return x * lax.rsqrt(jnp.mean(x * x, axis=-1, keepdims=True) + EPS)


def _adaln_kernel(c_ref, w_ref, b_ref, o_ref):
    c = c_ref[...]
    act = c * _sigmoid(c)
    o_ref[...] = jnp.dot(act, w_ref[...], preferred_element_type=F32,
                         precision=lax.Precision.HIGHEST) + b_ref[...]


def _adaln(c, ada_w, ada_b):
    B, D = c.shape
    N = ada_w.shape[1]
    tn = 512
    return pl.pallas_call(
        _adaln_kernel,
        out_shape=jax.ShapeDtypeStruct((B, N), F32),
        grid=(N // tn,),
        in_specs=[pl.BlockSpec((B, D), lambda j: (0, 0)),
                  pl.BlockSpec((D, tn), lambda j: (0, j)),
                  pl.BlockSpec((1, tn), lambda j: (0, j))],
        out_specs=pl.BlockSpec((B, tn), lambda j: (0, j)),
        compiler_params=_cparams("arbitrary"),
        name="adaln",
    )(c, ada_w, ada_b.reshape(1, N))


def _in_proj_kernel(x_ref, nw_ref, sh_ref, sc_ref, w_ref, o_ref, h_scr):
    @pl.when(pl.program_id(1) == 0)
    def _():
        y = _rms_scale(x_ref[...]) * nw_ref[...]
        h_scr[...] = (y * (1.0 + sc_ref[0]) + sh_ref[0]).astype(BF16)

    o_ref[...] = jnp.dot(h_scr[...], w_ref[...], preferred_element_type=F32)


def _in_proj(x2, norm_w, shift, scale, w_bf16, S):
    N, D = x2.shape
    W = w_bf16.shape[1]
    tm = min(1024, S)
    tn = 512
    per_b = S // tm
    return pl.pallas_call(
        _in_proj_kernel,
        out_shape=jax.ShapeDtypeStruct((N, W), F32),
        grid=(N // tm, W // tn),
        in_specs=[pl.BlockSpec((tm, D), lambda i, j: (i, 0)),
                  pl.BlockSpec((1, D), lambda i, j: (0, 0)),
                  pl.BlockSpec((1, 1, D), lambda i, j: (i // per_b, 0, 0)),
                  pl.BlockSpec((1, 1, D), lambda i, j: (i // per_b, 0, 0)),
                  pl.BlockSpec((D, tn), lambda i, j: (0, j))],
        out_specs=pl.BlockSpec((tm, tn), lambda i, j: (i, j)),
        scratch_shapes=[pltpu.VMEM((tm, D), BF16)],
        compiler_params=_cparams("parallel", "arbitrary"),
        name="in_proj",
    )(x2, norm_w.reshape(1, D), shift, scale, w_bf16)


def _group_meansq(x, gmat):
    hi, lo = _split_bf16(x * x, 2)
    return (jnp.dot(hi, gmat, preferred_element_type=F32)
            + jnp.dot(lo, gmat, preferred_element_type=F32))


def _rope(x, cos, sin_a, sin_b):
    n = x.shape[1]
    return x * cos + pltpu.roll(x, n - 16, 1) * sin_a + pltpu.roll(x, 16, 1) * sin_b


LOG2E = 1.4426950408889634
SHIFT_MARGIN = 1.02
SHIFT_LIMIT = 60.0


def _attn_kernel(q_ref, k_ref, v_ref, cos_ref, sa_ref, sb_ref, gq_ref, gk_ref,
                 qg_ref, kg_ref, o_ref, kt_scr, v_scr, kmax_scr, *, tq):
    i = pl.program_id(1)
    n_q = N_HEADS // N_KV_HEADS
    hd = HEAD_DIM

    @pl.when(i == 0)
    def _():
        k = k_ref[...]
        S = k.shape[0]
        kn = k * lax.rsqrt(_group_meansq(k, gk_ref[...]) + EPS) * kg_ref[...]
        kr = _rope(kn, cos_ref[...], sa_ref[...], sb_ref[...])
        k2max = jnp.max(_group_meansq(kr, gk_ref[...]) * hd, axis=0, keepdims=True)
        per_q = jnp.concatenate([k2max[:, g * hd:(g + 1) * hd] for g in range(N_KV_HEADS) for _ in range(n_q)],
                                axis=1)
        kmax_scr[...] = jnp.broadcast_to(jnp.sqrt(per_q), kmax_scr.shape)
        krt = kr.T
        one_row = jnp.where(lax.broadcasted_iota(jnp.int32, (hd, S), 0) == 0, 1.0, 0.0)
        v = v_ref[...]
        ones = jnp.ones((S, hd), F32)
        for g in range(N_KV_HEADS):
            kt_scr[g] = jnp.concatenate([krt[g * hd:(g + 1) * hd, :], one_row], axis=0).astype(BF16)
            v_scr[g] = jnp.concatenate([v[:, g * hd:(g + 1) * hd], ones], axis=1).astype(BF16)

    rows = pl.ds(pl.multiple_of(i * tq, tq), tq)
    tile = lambda t: jnp.concatenate([t] * (N_HEADS // 2), axis=1)
    q = q_ref[...]
    qn = q * lax.rsqrt(_group_meansq(q, gq_ref[...]) + EPS) * (qg_ref[...] * (hd ** -0.5 * LOG2E))
    qr = _rope(qn, tile(cos_ref[rows, :]), tile(sa_ref[rows, :]), tile(sb_ref[rows, :]))
    qnorm = jnp.sqrt(_group_meansq(qr, gq_ref[...]) * hd)
    bound = qnorm * kmax_scr[0:1, :] * SHIFT_MARGIN
    safe = jnp.max(bound) <= SHIFT_LIMIT
    qb = qr.astype(BF16)
    head_lane0 = jnp.bitwise_and(lax.broadcasted_iota(jnp.int32, bound.shape, 1), hd - 1) == 0
    nb = jnp.where(head_lane0, -bound, 0.0).astype(BF16)
    qs = []
    for g in range(N_KV_HEADS):
        heads = range(n_q * g, n_q * (g + 1))
        qs.append(jnp.concatenate(
            [jnp.concatenate([qb[:, h * hd:(h + 1) * hd], nb[:, h * hd:(h + 1) * hd]], axis=1) for h in heads],
            axis=0))

    def finish(acc):
        o = acc[:, :hd] / acc[:, hd:]
        return [o[r * tq:(r + 1) * tq, :] for r in range(n_q)]

    @pl.when(safe)
    def _():
        outs = []
        for g in range(N_KV_HEADS):
            s = jnp.dot(qs[g], kt_scr[g], preferred_element_type=F32)
            outs += finish(jnp.dot(jnp.exp2(s).astype(BF16), v_scr[g], preferred_element_type=F32))
        o_ref[...] = jnp.concatenate(outs, axis=1)

    @pl.when(jnp.logical_not(safe))
    def _():
        outs = []
        for g in range(N_KV_HEADS):
            s = jnp.dot(qs[g], kt_scr[g], preferred_element_type=F32)
            p = jnp.exp2(s - jnp.max(s, axis=-1, keepdims=True))
            outs += finish(jnp.dot(p.astype(BF16), v_scr[g], preferred_element_type=F32))
        o_ref[...] = jnp.concatenate(outs, axis=1)


def _rope_tables(S):
    half = HEAD_DIM // 2
    inv = ROPE_THETA ** (-jnp.arange(0, half, 2, dtype=F32) / half)
    t = jnp.arange(S, dtype=jnp.int32)
    ang_r = (t // GRID_W).astype(F32)[:, None] * inv
    ang_c = (t % GRID_W).astype(F32)[:, None] * inv
    zero = jnp.zeros_like(ang_r)
    cos = jnp.concatenate([jnp.cos(ang_r)] * 2 + [jnp.cos(ang_c)] * 2, axis=1)
    sin_a = jnp.concatenate([-jnp.sin(ang_r), zero, -jnp.sin(ang_c), zero], axis=1)
    sin_b = jnp.concatenate([zero, jnp.sin(ang_r), zero, jnp.sin(ang_c)], axis=1)
    two = lambda a: jnp.concatenate([a, a], axis=1)
    return two(cos), two(sin_a), two(sin_b)


def _group_matrix(n, group):
    idx = jnp.arange(n) // group
    return jnp.where(idx[:, None] == idx[None, :], 1.0 / group, 0.0).astype(BF16)


def _attention(proj3, q_gain, k_gain):
    B, S, _ = proj3.shape
    tq = 128
    cos, sin_a, sin_b = _rope_tables(S)
    qw = N_HEADS * HEAD_DIM
    kw = N_KV_HEADS * HEAD_DIM
    const = lambda shape: pl.BlockSpec(shape, lambda b, i: (0,) * len(shape))
    return pl.pallas_call(
        functools.partial(_attn_kernel, tq=tq),
        out_shape=jax.ShapeDtypeStruct((B, S, qw), F32),
        grid=(B, S // tq),
        in_specs=[pl.BlockSpec((None, tq, qw), lambda b, i: (b, i, COL_Q // qw)),
                  pl.BlockSpec((None, S, kw), lambda b, i: (b, 0, COL_K // kw)),
                  pl.BlockSpec((None, S, kw), lambda b, i: (b, 0, COL_V // kw)),
                  const((S, kw)), const((S, kw)), const((S, kw)),
                  const((qw, qw)), const((kw, kw)), const((1, qw)), const((1, kw))],
        out_specs=pl.BlockSpec((None, tq, qw), lambda b, i: (b, i, 0)),
        scratch_shapes=[pltpu.VMEM((N_KV_HEADS, 2 * HEAD_DIM, S), BF16),
                        pltpu.VMEM((N_KV_HEADS, S, 2 * HEAD_DIM), BF16),
                        pltpu.VMEM((8, qw), F32)],
        compiler_params=_cparams("parallel", "arbitrary"),
        name="attention",
    )(proj3, proj3, proj3, cos, sin_a, sin_b,
      _group_matrix(qw, HEAD_DIM), _group_matrix(kw, HEAD_DIM),
      jnp.tile(q_gain, N_HEADS).reshape(1, qw), jnp.tile(k_gain, N_KV_HEADS).reshape(1, kw))


def _conv_kernel(prev_ref, main_ref, next_ref, cw_ref, cb_ref, xs_ref, bc_ref, ext_scr, *, ts):
    i = pl.program_id(1)
    last = pl.num_programs(1) - 1
    ext_scr[0:8, :] = jnp.where(i > 0, prev_ref[...], 0.0)
    ext_scr[8:8 + ts, :] = main_ref[...]
    ext_scr[8 + ts:16 + ts, :] = jnp.where(i < last, next_ref[...], 0.0)
    pad = (CONV_W - 1) // 2
    acc = cb_ref[...] + ext_scr[pl.ds(8 - pad, ts), :] * cw_ref[0:1, :]
    for w in range(1, CONV_W):
        acc = acc + ext_scr[pl.ds(8 - pad + w, ts), :] * cw_ref[w:w + 1, :]
    u = acc * _sigmoid(acc)
    d_inner = xs_ref.shape[-1]
    xs_ref[...] = u[:, :d_inner]
    bc_ref[...] = u[:, d_inner:]


def _conv_silu(proj3, conv_w, conv_b):
    B, S, _ = proj3.shape
    C = conv_w.shape[1]
    d_inner = SSD_HEADS * SSD_HEAD_DIM
    ts = min(512, S)
    nb8 = S // 8
    cb = COL_XBC // C
    return pl.pallas_call(
        functools.partial(_conv_kernel, ts=ts),
        out_shape=(jax.ShapeDtypeStruct((B, S, d_inner), F32),
                   jax.ShapeDtypeStruct((B, S, C - d_inner), F32)),
        grid=(B, S // ts),
        in_specs=[pl.BlockSpec((None, 8, C), lambda b, i: (b, jnp.maximum(i * (ts // 8) - 1, 0), cb)),
                  pl.BlockSpec((None, ts, C), lambda b, i: (b, i, cb)),
                  pl.BlockSpec((None, 8, C), lambda b, i: (b, jnp.minimum((i + 1) * (ts // 8), nb8 - 1), cb)),
                  pl.BlockSpec((CONV_W, C), lambda b, i: (0, 0)),
                  pl.BlockSpec((1, C), lambda b, i: (0, 0))],
        out_specs=(pl.BlockSpec((None, ts, d_inner), lambda b, i: (b, i, 0)),
                   pl.BlockSpec((None, ts, C - d_inner), lambda b, i: (b, i, 0))),
        scratch_shapes=[pltpu.VMEM((ts + 16, C), F32)],
        compiler_params=_cparams("parallel", "arbitrary"),
        name="conv_silu",
    )(proj3, proj3, proj3, conv_w, conv_b.reshape(1, C))


def _ssd_direction(xs, bc, dt_raw, dtT_raw, bias_r, alog_r, bias_c, alog_c, h_ref, reverse):
    L = SSD_CHUNK
    hp = SSD_HEADS // SSD_GROUPS * SSD_HEAD_DIM
    dt = _softplus(dt_raw + bias_r)
    a = dt * (-jnp.exp(alog_r))
    dtT = _softplus(dtT_raw + bias_c)
    aT = dtT * (-jnp.exp(alog_c))
    li = lax.broadcasted_iota(jnp.int32, (L, L), 0)
    si = lax.broadcasted_iota(jnp.int32, (L, L), 1)
    incl = (si >= li) if reverse else (si <= li)
    inclT = (li >= si) if reverse else (li <= si)
    tri = jnp.where(incl, 1.0, 0.0).astype(BF16)
    triT = jnp.where(inclT, 1.0, 0.0).astype(BF16)
    p = sum(jnp.dot(tri, part, preferred_element_type=F32) for part in _split_bf16(a, 3))
    pT = sum(jnp.dot(part, triT, preferred_element_type=F32) for part in _split_bf16(aT, 3))
    edge = 0 if reverse else L - 1
    tot = p[edge:edge + 1, :]
    dec = jnp.exp(tot - p)
    pe = jnp.exp(p)

    hh = lax.broadcasted_iota(jnp.int32, (2 * SSD_HEADS, SSD_HEADS * SSD_HEAD_DIM), 0) % SSD_HEADS
    jj = lax.broadcasted_iota(jnp.int32, (2 * SSD_HEADS, SSD_HEADS * SSD_HEAD_DIM), 1) // SSD_HEAD_DIM
    expand_m = jnp.where(hh == jj, 1.0, 0.0).astype(BF16)

    def expand(v):
        hi, lo = _split_bf16(v, 2)
        return jnp.dot(jnp.concatenate([hi, lo], axis=1), expand_m, preferred_element_type=F32)

    dt_x = expand(dt)
    dtdec_x = expand(dt * dec)
    pe_x = expand(pe)
    X = (xs * dt_x).astype(BF16)
    Xs = (xs * dtdec_x).astype(BF16)
    lane = lax.broadcasted_iota(jnp.int32, (L, 2 * SSD_HEAD_DIM), 1)
    ns = SSD_GROUPS * SSD_STATE
    y_groups = []
    for g in range(SSD_GROUPS):
        Bg = bc[:, g * SSD_STATE:(g + 1) * SSD_STATE].astype(BF16)
        Cg = bc[:, ns + g * SSD_STATE:ns + (g + 1) * SSD_STATE].astype(BF16)
        cb = lax.dot_general(Cg, Bg, (((1,), (1,)), ((), ())), preferred_element_type=F32)
        ys = []
        for pr in range(SSD_HEADS // SSD_GROUPS // 2):
            h0 = g * (SSD_HEADS // SSD_GROUPS) + 2 * pr
            ms = []
            for h in (h0, h0 + 1):
                diff = p[:, h:h + 1] - pT[h:h + 1, :]
                ms.append(cb * jnp.exp(jnp.where(incl, diff, -jnp.inf)))
            lhs = jnp.concatenate(ms, axis=1).astype(BF16)
            xp = X[:, h0 * SSD_HEAD_DIM:(h0 + 2) * SSD_HEAD_DIM]
            zero = jnp.zeros_like(xp)
            rhs = jnp.concatenate([jnp.where(lane < SSD_HEAD_DIM, xp, zero),
                                   jnp.where(lane >= SSD_HEAD_DIM, xp, zero)], axis=0)
            ys.append(jnp.dot(lhs, rhs, preferred_element_type=F32))
        y_diag = jnp.concatenate(ys, axis=1)
        h_old = h_ref[g]
        y_off = jnp.dot(Cg, h_old.astype(BF16), preferred_element_type=F32) * pe_x[:, g * hp:(g + 1) * hp]
        st = lax.dot_general(Bg, Xs[:, g * hp:(g + 1) * hp], (((0,), (0,)), ((), ())),
                             preferred_element_type=F32)
        h_ref[g] = h_old * pe_x[edge:edge + 1, g * hp:(g + 1) * hp] + st
        y_groups.append(y_diag + y_off)
    return jnp.concatenate(y_groups, axis=1)


def _ssd_kernel(xsf_ref, xsb_ref, bcf_ref, bcb_ref, dtf_ref, dtb_ref, dtTf_ref, dtTb_ref,
                bias_r_ref, alog_r_ref, bias_c_ref, alog_c_ref, yf_ref, yb_ref, h_scr):
    @pl.when(pl.program_id(1) == 0)
    def _():
        h_scr[...] = jnp.zeros_like(h_scr)

    H = SSD_HEADS
    yf_ref[...] = _ssd_direction(
        xsf_ref[...], bcf_ref[...], dtf_ref[:, 0:H], dtTf_ref[0:H, :],
        bias_r_ref[:, 0:H], alog_r_ref[:, 0:H], bias_c_ref[0:H, :], alog_c_ref[0:H, :],
        h_scr.at[0], reverse=False)
    yb_ref[...] = _ssd_direction(
        xsb_ref[...], bcb_ref[...], dtb_ref[:, H:2 * H], dtTb_ref[H:2 * H, :],
        bias_r_ref[:, H:2 * H], alog_r_ref[:, H:2 * H], bias_c_ref[H:2 * H, :], alog_c_ref[H:2 * H, :],
        h_scr.at[1], reverse=True)


def _ssd_scan(xs, bc, proj3, dt_bias, a_log):
    B, S, d_inner = xs.shape
    L = SSD_CHUNK
    nc = S // L
    H2 = 2 * SSD_HEADS
    dtT = jnp.swapaxes(proj3[:, :, COL_DT:COL_DT + H2], 1, 2)
    fwd = lambda b, c: (b, c, 0)
    bwd = lambda b, c: (b, nc - 1 - c, 0)
    const = lambda shape: pl.BlockSpec(shape, lambda b, c: (0,) * len(shape))
    hp = SSD_HEADS // SSD_GROUPS * SSD_HEAD_DIM
    return pl.pallas_call(
        _ssd_kernel,
        out_shape=(jax.ShapeDtypeStruct((B, S, d_inner), F32),
                   jax.ShapeDtypeStruct((B, S, d_inner), F32)),
        grid=(B, nc),
        in_specs=[pl.BlockSpec((None, L, d_inner), fwd),
                  pl.BlockSpec((None, L, d_inner), bwd),
                  pl.BlockSpec((None, L, bc.shape[-1]), fwd),
                  pl.BlockSpec((None, L, bc.shape[-1]), bwd),
                  pl.BlockSpec((None, L, 128), lambda b, c: (b, c, COL_DT // 128)),
                  pl.BlockSpec((None, L, 128), lambda b, c: (b, nc - 1 - c, COL_DT // 128)),
                  pl.BlockSpec((None, H2, L), lambda b, c: (b, 0, c)),
                  pl.BlockSpec((None, H2, L), lambda b, c: (b, 0, nc - 1 - c)),
                  const((1, H2)), const((1, H2)), const((H2, 1)), const((H2, 1))],
        out_specs=(pl.BlockSpec((None, L, d_inner), fwd),
                   pl.BlockSpec((None, L, d_inner), bwd)),
        scratch_shapes=[pltpu.VMEM((2, SSD_GROUPS, SSD_STATE, hp), F32)],
        compiler_params=_cparams("parallel", "arbitrary"),
        name="ssd_scan",
    )(xs, xs, bc, bc, proj3, proj3, dtT, dtT,
      dt_bias.reshape(1, H2), a_log.reshape(1, H2), dt_bias.reshape(H2, 1), a_log.reshape(H2, 1))


def _merge_kernel(yf_ref, yb_ref, xs_ref, z_ref, ga_ref, gs_ref, att_ref, x_ref,
                  dskip_ref, snw_ref, wa_ref, ws_ref, wo_ref, g1_ref,
                  n2w_ref, sh2_ref, sc2_ref, wq_ref,
                  x1_ref, h2t_ref, qp_ref):
    y = yf_ref[...] + yb_ref[...] + xs_ref[...] * dskip_ref[...]
    z = z_ref[...]
    ssd = _rms_scale(y * (z * _sigmoid(z))) * snw_ref[...]
    up_a = jnp.dot(att_ref[...].astype(BF16), wa_ref[...], preferred_element_type=F32)
    up_s = jnp.dot(ssd.astype(BF16), ws_ref[...], preferred_element_type=F32)
    merged = _sigmoid(ga_ref[...]) * up_a + _sigmoid(gs_ref[...]) * up_s
    mix = jnp.dot(merged.astype(BF16), wo_ref[...], preferred_element_type=F32)
    x1 = x_ref[...] + g1_ref[0] * mix
    x1_ref[...] = x1
    h2 = (_rms_scale(x1) * n2w_ref[...]) * (1.0 + sc2_ref[0]) + sh2_ref[0]
    h2b = h2.astype(BF16)
    h2t_ref[...] = h2.T.astype(BF16)
    qp_ref[...] = jnp.dot(h2b, wq_ref[...], preferred_element_type=F32)


def _merge(yf, yb, xs, proj, att, x2, d_skip_x, ssd_norm_w, wa, ws, wo, g1, norm2_w, sh2, sc2, wq, S):
    N, D = x2.shape
    tm = min(256, S)
    per_b = S // tm
    aw = att.shape[1]
    QW = wq.shape[1]
    row = lambda w, c=0: pl.BlockSpec((tm, w), lambda i: (i, c))
    const = lambda shape: pl.BlockSpec(shape, lambda i: (0,) * len(shape))
    perb = pl.BlockSpec((1, 1, D), lambda i: (i // per_b, 0, 0))
    return pl.pallas_call(
        _merge_kernel,
        out_shape=(jax.ShapeDtypeStruct((N, D), F32),
                   jax.ShapeDtypeStruct((D, N), BF16),
                   jax.ShapeDtypeStruct((N, QW), F32)),
        grid=(N // tm,),
        in_specs=[row(D), row(D), row(D),
                  row(D, COL_Z // D), row(D, COL_GATE // D), row(D, COL_GATE // D + 1),
                  row(aw), row(D),
                  const((1, D)), const((1, D)),
                  const((aw, D)), const((D, D)), const((D, D)), perb,
                  const((1, D)), perb, perb, const((D, QW))],
        out_specs=(row(D), pl.BlockSpec((D, tm), lambda i: (0, i)), row(QW)),
        compiler_params=_cparams("parallel"),
        name="merge",
    )(yf, yb, xs, proj, proj, proj, att, x2,
      d_skip_x, ssd_norm_w.reshape(1, D), wa, ws, wo, g1,
      norm2_w.reshape(1, D), sh2, sc2, wq)


def _top_values(s, k):
    vals = []
    for _ in range(k):
        m = jnp.max(s, axis=0, keepdims=True)
        vals.append(m)
        s = jnp.where(s >= m, -jnp.inf, s)
    return vals


def _peer_route_kernel(q_ref, k1h_ref, k1l_ref, k2h_ref, k2l_ref, n1_ref, e1z_ref, r2_ref, e2_ref):
    K = PEER_TOPK
    dh = k1h_ref.shape[-1]
    q = q_ref[...]
    nt = (((1,), (1,)), ((), ()))

    def scores_t(qf, kh, kl):
        qh, ql = _split_bf16(qf, 2)
        dg = functools.partial(lax.dot_general, dimension_numbers=nt, preferred_element_type=F32)
        return dg(kh, qh) + dg(kl, qh) + dg(kh, ql)

    s1 = scores_t(q[:, :dh], k1h_ref[...], k1l_ref[...])
    s2 = scores_t(q[:, dh:], k2h_ref[...], k2l_ref[...])
    v1 = _top_values(s1, K)
    v2 = _top_values(s2, K)
    tt = s1.shape[1]
    inf = jnp.inf
    v2_all = jnp.concatenate(v2, axis=0)
    v2_8 = jnp.concatenate(v2[:8], axis=0)
    v1_hi = jnp.concatenate(v1[8:], axis=0)
    rowid = lax.broadcasted_iota(jnp.int32, (8, tt), 0)
    pieces = [v1[0] + v2_all]
    for a in range(1, 8):
        pieces.append(jnp.where(rowid < K // (a + 1), v1[a] + v2_8, -inf))
    pieces.append(v1_hi + v2[0])
    cand = jnp.concatenate(pieces, axis=0)
    tau = _top_values(cand, K)[-1]
    top = v1[0] + v2[0]
    zsum = jnp.sum(jnp.where(cand >= tau, jnp.exp(cand - top), 0.0), axis=0, keepdims=True)
    count = lambda piece: jnp.sum(jnp.where(piece >= tau, 1.0, 0.0), axis=0, keepdims=True)
    n_rank = [count(pieces[a]) for a in range(8)]
    n_hi = jnp.where(pieces[8] >= tau, 1.0, 0.0)
    n_rank += [n_hi[a - 8:a - 7, :] for a in range(8, K)]
    n1 = jnp.zeros(s1.shape, F32)
    for a in range(K):
        n1 = jnp.where(s1 == v1[a], n_rank[a], n1)
    r2 = jnp.zeros(s2.shape, F32)
    for b in range(K):
        r2 = r2 + jnp.where(s2 < v2[b], 1.0, 0.0)
    n1_ref[...] = n1.astype(BF16)
    e1z_ref[...] = (jnp.exp(s1 - v1[0]) * (1.0 / zsum)).astype(BF16)
    r2_ref[...] = pltpu.bitcast(r2.astype(BF16), jnp.int32)
    e2_ref[...] = pltpu.bitcast(jnp.exp(s2 - v2[0]).astype(BF16), jnp.int32)


def _peer_route(qp, keys1, keys2):
    N = qp.shape[0]
    Hp, nk, dh = keys1.shape
    tt = 256
    k1h, k1l = _split_bf16(keys1, 2)
    k2h, k2l = _split_bf16(keys2, 2)
    kspec = pl.BlockSpec((None, nk, dh), lambda i, h: (h, 0, 0))
    ospec = pl.BlockSpec((None, nk, tt), lambda i, h: (h, 0, i))
    oshape = jax.ShapeDtypeStruct((Hp, nk, N), BF16)
    pspec = pl.BlockSpec((None, nk // 2, tt), lambda i, h: (h, 0, i))
    pshape = jax.ShapeDtypeStruct((Hp, nk // 2, N), jnp.int32)
    return pl.pallas_call(
        _peer_route_kernel,
        out_shape=(oshape, oshape, pshape, pshape),
        grid=(N // tt, Hp),
        in_specs=[pl.BlockSpec((tt, 2 * dh), lambda i, h: (i, h)), kspec, kspec, kspec, kspec],
        out_specs=(ospec, ospec, pspec, pspec),
        compiler_params=_cparams("parallel", "arbitrary"),
        name="peer_route",
    )(qp, k1h, k1l, k2h, k2l)


_GELU_C0 = -2.0 * math.sqrt(2.0 / math.pi) * LOG2E
_GELU_C1 = _GELU_C0 * 0.044715


def _gelu_tanh(x):
    return x / (1.0 + jnp.exp2(x * (x * x * _GELU_C1 + _GELU_C0)))


def _peer_weighted_act(a_ref, n1_ref, e1z_ref, r2_ref, e2_ref, wg_ref, slab0, k_range):
    nk = PEER_KEYS
    T = a_ref.shape[1]
    sub = BF16_SUBLANES
    zero = jnp.zeros((), BF16)
    for k in k_range:
        for c in range(T // 128):
            cols = slice(c * 128, (c + 1) * 128)
            w = [None] * (nk // sub)
            for h in range(PEER_HEADS):
                row = slice(slab0 + k, slab0 + k + 1)
                n = jnp.broadcast_to(n1_ref[h, :, cols].astype(F32)[row], (sub, 128)).astype(BF16)
                ez = jnp.broadcast_to(e1z_ref[h, :, cols].astype(F32)[row], (sub, 128)).astype(BF16)
                for j in range(nk // sub):
                    r2 = pltpu.bitcast(r2_ref[h, j], BF16)[:, cols]
                    e2 = pltpu.bitcast(e2_ref[h, j], BF16)[:, cols]
                    t = jnp.where(r2 < n, e2, zero) * ez
                    w[j] = t if w[j] is None else w[j] + t
            for j in range(nk // sub):
                rows = slice(k * nk + j * sub, k * nk + (j + 1) * sub)
                words = slice((k * nk + j * sub) // 2, (k * nk + (j + 1) * sub) // 2)
                wg_ref[words, cols] = pltpu.bitcast(w[j] * _gelu_tanh(a_ref[rows, cols]).astype(BF16), jnp.int32)


def _peer_dense_kernel(ht_ref, u_ref, vt_ref, n1p_ref, e1zp_ref, n1c_ref, e1zc_ref,
                       r2p_ref, e2p_ref, r2c_ref, e2c_ref, y_ref, a0, a1, wg0, wg1, *, blocks_per_tile):
    g = pl.program_id(0)
    half = a0.shape[0]
    slabs = half // PEER_KEYS

    @pl.when(g == 0)
    def _():
        a1[...] = jnp.zeros_like(a1)
        wg0[...] = jnp.zeros_like(wg0)

    @pl.when(jnp.maximum(g - 1, 0) % blocks_per_tile == 0)
    def _():
        y_ref[...] = jnp.zeros_like(y_ref)

    ht = ht_ref[...]
    a0[...] = jnp.dot(u_ref[0:half, :], ht, preferred_element_type=F32)
    _peer_weighted_act(a1, n1p_ref, e1zp_ref, r2p_ref, e2p_ref, wg1, slabs, range(slabs))
    y_ref[...] += jnp.dot(vt_ref[:, 0:half], pltpu.bitcast(wg0[...], BF16), preferred_element_type=F32)
    a1[...] = jnp.dot(u_ref[half:2 * half, :], ht, preferred_element_type=F32)
    _peer_weighted_act(a0, n1c_ref, e1zc_ref, r2c_ref, e2c_ref, wg0, 0, range(slabs))
    y_ref[...] += jnp.dot(vt_ref[:, half:2 * half], pltpu.bitcast(wg1[...], BF16), preferred_element_type=F32)


def _peer_dense(h2t, u_bf16, vt_bf16, n1, e1z, r2, e2):
    D, N = h2t.shape
    E = u_bf16.shape[0]
    Hp, nk, _ = n1.shape
    T = min(512, N)
    Eb = 1024
    slabs = Eb // nk
    n_t, n_b = N // T, E // Eb
    steps = n_t * n_b + 1
    tile = lambda g: jnp.minimum(g // n_b, n_t - 1)
    prev = lambda g: jnp.maximum(g - 1, 0)
    rows_p = pl.BlockSpec((Hp, None, slabs, T), lambda g: (0, prev(g) % n_b, 0, tile(prev(g))))
    rows_c = pl.BlockSpec((Hp, None, slabs, T), lambda g: (0, g % n_b, 0, tile(g)))
    sub = BF16_SUBLANES
    res_p = pl.BlockSpec((Hp, nk // sub, sub // 2, T), lambda g: (0, 0, 0, tile(prev(g))))
    res_c = pl.BlockSpec((Hp, nk // sub, sub // 2, T), lambda g: (0, 0, 0, tile(g)))
    n1r = n1.reshape(Hp, nk // slabs, slabs, N)
    e1zr = e1z.reshape(Hp, nk // slabs, slabs, N)
    r2 = r2.reshape(Hp, nk // sub, sub // 2, N)
    e2 = e2.reshape(Hp, nk // sub, sub // 2, N)
    return pl.pallas_call(
        functools.partial(_peer_dense_kernel, blocks_per_tile=n_b),
        out_shape=jax.ShapeDtypeStruct((D, N), F32),
        grid=(steps,),
        in_specs=[pl.BlockSpec((D, T), lambda g: (0, tile(g))),
                  pl.BlockSpec((Eb, D), lambda g: (g % n_b, 0)),
                  pl.BlockSpec((D, Eb), lambda g: (0, prev(g) % n_b)),
                  rows_p, rows_p, rows_c, rows_c, res_p, res_p, res_c, res_c],
        out_specs=pl.BlockSpec((D, T), lambda g: (0, tile(prev(g)))),
        scratch_shapes=[pltpu.VMEM((Eb // 2, T), F32), pltpu.VMEM((Eb // 2, T), F32),
                        pltpu.VMEM((Eb // 4, T), jnp.int32), pltpu.VMEM((Eb // 4, T), jnp.int32)],
        compiler_params=_cparams("arbitrary"),
        name="peer_dense",
    )(h2t, u_bf16, vt_bf16, n1r, e1zr, n1r, e1zr, r2, e2, r2, e2)


def _final_kernel(x1_ref, pet_ref, g2_ref, w_ref, o_ref):
    o_ref[...] = _rms_scale(x1_ref[...] + g2_ref[0] * pet_ref[...].T) * w_ref[...]


def _final(x1, peer_out_t, g2, final_w, S):
    N, D = x1.shape
    tm = min(512, S)
    per_b = S // tm
    row = pl.BlockSpec((tm, D), lambda i: (i, 0))
    return pl.pallas_call(
        _final_kernel,
        out_shape=jax.ShapeDtypeStruct((N, D), F32),
        grid=(N // tm,),
        in_specs=[row, pl.BlockSpec((D, tm), lambda i: (0, i)),
                  pl.BlockSpec((1, 1, D), lambda i: (i // per_b, 0, 0)),
                  pl.BlockSpec((1, D), lambda i: (0, 0))],
        out_specs=row,
        compiler_params=_cparams("parallel"),
        name="final_norm",
    )(x1, peer_out_t, g2, final_w.reshape(1, D))


def _reorder_w_in(w_in):
    qw = N_HEADS * HEAD_DIM
    kw = N_KV_HEADS * HEAD_DIM
    d_inner = SSD_HEADS * SSD_HEAD_DIM
    xbc_w = d_inner + 2 * SSD_GROUPS * SSD_STATE
    o1 = qw
    o2 = o1 + kw
    o3 = o2 + kw
    o4 = o3 + d_inner
    o5 = o4 + xbc_w
    o6 = o5 + 2 * SSD_HEADS
    w = jnp.concatenate([w_in[:, o3:o4], w_in[:, o6:], w_in[:, o4:o5], w_in[:, :o3], w_in[:, o5:o6]], axis=1)
    return jnp.pad(w, ((0, 0), (0, PROJ_W - w.shape[1]))).astype(BF16)


def kernel(x, c, ada_w, ada_b, norm1_w, w_in, q_gain, k_gain, conv_w, conv_b, dt_bias_f, dt_bias_b,
           a_log_f, a_log_b, d_skip, ssd_norm_w, w_attn_up, w_ssd_up, w_out, norm2_w, peer_w_query,
           peer_keys1, peer_keys2, peer_u, peer_v, final_norm_w):
    B, S, D = x.shape
    assert ada_w.shape[0] == 1, "single-layer block"
    N = B * S
    x2 = x.reshape(N, D)
    mod = _adaln(c, ada_w[0], ada_b[0]).reshape(B, 6, 1, D)
    sh1, sc1, g1, sh2, sc2, g2 = (mod[:, k] for k in range(6))
    proj = _in_proj(x2, norm1_w[0], sh1, sc1, _reorder_w_in(w_in[0]), S)
    proj3 = proj.reshape(B, S, PROJ_W)
    att = _attention(proj3, q_gain[0], k_gain[0])
    xs, bc = _conv_silu(proj3, conv_w[0], conv_b[0])
    yf, yb = _ssd_scan(xs, bc, proj3,
                       jnp.concatenate([dt_bias_f[0], dt_bias_b[0]]),
                       jnp.concatenate([a_log_f[0], a_log_b[0]]))
    x1, h2t, qp = _merge(
        yf.reshape(N, -1), yb.reshape(N, -1), xs.reshape(N, -1), proj, att.reshape(N, -1), x2,
        jnp.repeat(d_skip[0], SSD_HEAD_DIM).reshape(1, -1), ssd_norm_w[0],
        w_attn_up[0].astype(BF16), w_ssd_up[0].astype(BF16), w_out[0].astype(BF16), g1,
        norm2_w[0], sh2, sc2, peer_w_query[0].astype(BF16), S)
    n1, e1z, r2, e2 = _peer_route(qp, peer_keys1[0], peer_keys2[0])
    yt = _peer_dense(h2t, peer_u[0].astype(BF16), peer_v[0].T.astype(BF16), n1, e1z, r2, e2)
    out = _final(x1, yt, g2, final_norm_w, S)
    return out.reshape(B, S, D)
```
